```python
import jax, jax.numpy as jnp
from jax import lax
import numpy as np

D_MODEL = 2048
BATCH = 4
SEQ = 2048
DEPTH = 2

GRID_W = 64
CTX_LEN = 256
CONV_W = 1024
CONV_K = 3
MLA_HEADS = 8
MLA_NOPE = 128
MLA_ROPE = 64
MLA_V = 128
Q_LORA = 512
KV_LORA = 256
MLA_W = MLA_HEADS * MLA_V
MLA_SCALE = (MLA_NOPE + MLA_ROPE) ** -0.5
NA_HEADS = 16
NA_HD = 64
NA_W = NA_HEADS * NA_HD
NA_KH = 8
NA_KW = 16
NA_SCALE = NA_HD ** -0.5
FNET_GROUPS = 4
FNET_GW = 256
FNET_W = FNET_GROUPS * FNET_GW

N_BRANCH = 4
Q_BLOCK = 128
ROPE_THETA = 10000.0
EPS = 1e-6
IN_WIDTHS = (KV_LORA + MLA_ROPE, NA_W, NA_W, Q_LORA, NA_W, CONV_W, CONV_W, CONV_W, FNET_W,
             CONV_W, MLA_W, NA_W, FNET_W, N_BRANCH * D_MODEL)
KV_COLS = KV_LORA + MLA_ROPE + 2 * NA_W
N_IN = sum(IN_WIDTHS)

kernel_name = 'hybrid_parallel_mixer_diffusion_block'


def _split(h, widths):
    offs = np.cumsum(widths)[:-1].tolist()
    return jnp.split(h, offs, axis=-1)


def _rmsnorm(x, g):
    xf = x.astype(jnp.float32)
    y = xf * lax.rsqrt(jnp.mean(xf * xf, axis=-1, keepdims=True) + EPS)
    return (y * g.astype(jnp.float32)).astype(x.dtype)


def _heads(t, h, d):
    return t.reshape(t.shape[0], t.shape[1], h, d)


def _axial_rope(x):
    n = x.shape[1]
    nf = MLA_ROPE // 4
    t = jnp.arange(n, dtype=jnp.int32)
    pos = jnp.stack([t // GRID_W, t % GRID_W], axis=-1).astype(jnp.float32)
    inv = ROPE_THETA ** (-jnp.arange(nf, dtype=jnp.float32) / nf)
    ang = pos[:, :, None] * inv
    cos = jnp.cos(ang)[None, :, None]
    sin = jnp.sin(ang)[None, :, None]
    xr = x.astype(jnp.float32).reshape(x.shape[:-1] + (2, 2, nf))
    a, b = xr[..., 0, :], xr[..., 1, :]
    out = jnp.stack([a * cos - b * sin, a * sin + b * cos], axis=-2)
    return out.reshape(x.shape).astype(x.dtype)


def _dense_attention(q, k, v, scale):
    b, s, h, d = q.shape
    nb = s // Q_BLOCK
    qb = q.reshape(b, nb, Q_BLOCK, h, d).swapaxes(0, 1)

    def one(qi):
        sc = jnp.einsum('bqhd,bkhd->bhqk', qi, k, preferred_element_type=jnp.float32) * scale
        p = jax.nn.softmax(sc, axis=-1).astype(v.dtype)
        return jnp.einsum('bhqk,bkhv->bqhv', p, v)

    out = lax.map(one, qb)
    return out.swapaxes(0, 1).reshape(b, s, h, v.shape[-1])


def _neighbourhood_attention(q, k, v, kc, vc, rpb):
    b, s, h, d = q.shape
    rows = s // GRID_W
    kh = min(NA_KH, rows)
    kw = NA_KW
    qg = q.reshape(b, rows, GRID_W, h, d).swapaxes(0, 1)
    kg = k.reshape(b, rows, GRID_W, h, d)
    vg = v.reshape(b, rows, GRID_W, h, d)
    row_start = jnp.clip(jnp.arange(rows, dtype=jnp.int32) - kh // 2, 0, rows - kh)
    cols = jnp.arange(GRID_W, dtype=jnp.int32)
    col_idx = jnp.clip(cols - kw // 2, 0, GRID_W - kw)[:, None] + jnp.arange(kw, dtype=jnp.int32)
    col_off = col_idx - cols[:, None] + (NA_KW - 1)

    def one(args):
        r, qr = args
        rs = row_start[r]
        kn = lax.dynamic_slice_in_dim(kg, rs, kh, axis=1)[:, :, col_idx]
        vn = lax.dynamic_slice_in_dim(vg, rs, kh, axis=1)[:, :, col_idx]
        row_off = rs + jnp.arange(kh, dtype=jnp.int32) - r + (NA_KH - 1)
        bias = rpb[:, row_off[None, :, None], col_off[:, None, :]].astype(jnp.float32)
        s_nb = jnp.einsum('bchd,bicjhd->bhcij', qr, kn, preferred_element_type=jnp.float32) * NA_SCALE + bias
        s_cx = jnp.einsum('bchd,bkhd->bhck', qr, kc, preferred_element_type=jnp.float32) * NA_SCALE
        logits = jnp.concatenate([s_nb.reshape(b, h, GRID_W, kh * kw), s_cx], axis=-1)
        p = jax.nn.softmax(logits, axis=-1).astype(v.dtype)
        p_nb = p[..., :kh * kw].reshape(b, h, GRID_W, kh, kw)
        p_cx = p[..., kh * kw:]
        return (jnp.einsum('bhcij,bicjhd->bchd', p_nb, vn)
                + jnp.einsum('bhck,bkhd->bchd', p_cx, vc))

    out = lax.map(one, (jnp.arange(rows, dtype=jnp.int32), qg))
    return out.swapaxes(0, 1).reshape(b, s, h * d)


def _short_conv(z, w):
    return lax.conv_general_dilated(z, w.astype(z.dtype)[:, None, :], (1,),
                                    [(CONV_K // 2, CONV_K // 2)],
                                    dimension_numbers=('NWC', 'WIO', 'NWC'),
                                    feature_group_count=z.shape[-1])


def _fourier(v):
    b, n, _ = v.shape
    f = jnp.fft.fft2(v.astype(jnp.float32).reshape(b, n, FNET_GROUPS, FNET_GW), axes=(1, 3), norm='ortho')
    return f.real.reshape(b, n, FNET_W).astype(v.dtype)


def _mla_kv(h, g_kv, w_ukv, rotate):
    b, n, _ = h.shape
    c_kv, k_r = _split(h, (KV_LORA, MLA_ROPE))
    kv = (_rmsnorm(c_kv, g_kv) @ w_ukv).reshape(b, n, MLA_HEADS, MLA_NOPE + MLA_V)
    k_nope, v = _split(kv, (MLA_NOPE, MLA_V))
    k_r = k_r[:, :, None, :]
    if rotate:
        k_r = _axial_rope(k_r)
    k = jnp.concatenate([k_nope, jnp.broadcast_to(k_r, (b, n, MLA_HEADS, MLA_ROPE))], axis=-1)
    return k, v


def _mla_q(h, g_q, w_uq, rotate):
    b, n, _ = h.shape
    q = (_rmsnorm(h, g_q) @ w_uq).reshape(b, n, MLA_HEADS, MLA_NOPE + MLA_ROPE)
    q_nope, q_r = _split(q, (MLA_NOPE, MLA_ROPE))
    if rotate:
        q_r = _axial_rope(q_r)
    return jnp.concatenate([q_nope, q_r], axis=-1)


def _combine(parts, mla_o, na_o, conv_w, w_p_conv, w_p_mla, w_p_na, w_p_fnet, w_out):
    cb, cc, cx, fv, g_cv, g_ml, g_na, g_fn, mg = parts[5:]
    conv_o = cb * _short_conv(cc * cx, conv_w)
    fn_o = _fourier(fv)
    b, n, _ = mg.shape
    gates = jax.nn.sigmoid(mg).reshape(b, n, N_BRANCH, D_MODEL)
    branches = ((conv_o, g_cv, w_p_conv), (mla_o, g_ml, w_p_mla), (na_o, g_na, w_p_na), (fn_o, g_fn, w_p_fnet))
    m = gates[:, :, 0] * ((conv_o * jax.nn.silu(g_cv)) @ w_p_conv)
    for i in range(1, N_BRANCH):
        o, g, w = branches[i]
        m = m + gates[:, :, i] * ((o * jax.nn.silu(g)) @ w)
    return m @ w_out


def _layer(xc, xl, c, c_ctx, g_pre, g_post, w_ada, b_ada, w_in, g_q, g_kv, w_uq, w_ukv,
           conv_w, na_rpb, w_p_conv, w_p_mla, w_p_na, w_p_fnet, w_out, ctx_out):
    shl, scl, gtl = jnp.split(jax.nn.silu(c) @ w_ada + b_ada, 3, axis=-1)
    shc, scc, gtc = jnp.split(jax.nn.silu(c_ctx) @ w_ada + b_ada, 3, axis=-1)
    ul = _rmsnorm(xl, g_pre) * (1 + scl[:, None]) + shl[:, None]
    uc = _rmsnorm(xc, g_pre) * (1 + scc) + shc
    b, s, _ = xl.shape

    pl = _split(ul @ w_in, IN_WIDTHS)
    if ctx_out:
        pc = _split(uc @ w_in, IN_WIDTHS)
    else:
        pc = _split(uc @ w_in[:, :KV_COLS], IN_WIDTHS[:3])

    kc_m, vc_m = _mla_kv(pc[0], g_kv, w_ukv, False)
    kc_n = _heads(pc[1], NA_HEADS, NA_HD)
    vc_n = _heads(pc[2], NA_HEADS, NA_HD)

    kl_m, vl_m = _mla_kv(pl[0], g_kv, w_ukv, True)
    ql_m = _mla_q(pl[3], g_q, w_uq, True)
    mla_l = _dense_attention(ql_m, jnp.concatenate([kc_m, kl_m], axis=1),
                             jnp.concatenate([vc_m, vl_m], axis=1), MLA_SCALE).reshape(b, s, MLA_W)
    na_l = _neighbourhood_attention(_heads(pl[4], NA_HEADS, NA_HD), _heads(pl[1], NA_HEADS, NA_HD),
                                    _heads(pl[2], NA_HEADS, NA_HD), kc_n, vc_n, na_rpb)
    yl = _combine(pl, mla_l, na_l, conv_w, w_p_conv, w_p_mla, w_p_na, w_p_fnet, w_out)
    xl_new = xl + gtl[:, None] * _rmsnorm(yl, g_post)

    if not ctx_out:
        return None, xl_new
    qc_m = _mla_q(pc[3], g_q, w_uq, False)
    mla_c = _dense_attention(qc_m, kc_m, vc_m, MLA_SCALE).reshape(xc.shape[0], xc.shape[1], MLA_W)
    na_c = _dense_attention(_heads(pc[4], NA_HEADS, NA_HD), kc_n, vc_n, NA_SCALE).reshape(xc.shape[0], xc.shape[1], NA_W)
    yc = _combine(pc, mla_c, na_c, conv_w, w_p_conv, w_p_mla, w_p_na, w_p_fnet, w_out)
    xc_new = xc + gtc * _rmsnorm(yc, g_post)
    return xc_new, xl_new


def setup_inputs(seed: int = 0) -> dict:
    key = jax.random.key(seed)
    ks = jax.random.split(key, 20)

    def nrm(k, shape, scale):
        return jax.random.normal(k, shape, jnp.float32) * scale

    return {
        'x': nrm(ks[0], (BATCH, SEQ, D_MODEL), 1.0),
        'c': nrm(ks[1], (BATCH, D_MODEL), 1.0),
        'ctx': nrm(ks[2], (BATCH, CTX_LEN, D_MODEL), 1.0),
        'c_ctx': nrm(ks[3], (D_MODEL,), 1.0),
        'g_pre': 1.0 + nrm(ks[4], (DEPTH, D_MODEL), 0.05),
        'g_post': 1.0 + nrm(ks[5], (DEPTH, D_MODEL), 0.05),
        'w_ada': nrm(ks[6], (DEPTH, D_MODEL, 3 * D_MODEL), 0.5 * D_MODEL ** -0.5),
        'b_ada': nrm(ks[7], (DEPTH, 3 * D_MODEL), 0.02),
        'w_in': nrm(ks[8], (DEPTH, D_MODEL, N_IN), D_MODEL ** -0.5),
        'g_q': 1.0 + nrm(ks[9], (DEPTH, Q_LORA), 0.05),
        'g_kv': 1.0 + nrm(ks[10], (DEPTH, KV_LORA), 0.05),
        'w_uq': nrm(ks[11], (DEPTH, Q_LORA, MLA_HEADS * (MLA_NOPE + MLA_ROPE)), Q_LORA ** -0.5),
        'w_ukv': nrm(ks[12], (DEPTH, KV_LORA, MLA_HEADS * (MLA_NOPE + MLA_V)), KV_LORA ** -0.5),
        'conv_w': nrm(ks[13], (DEPTH, CONV_K, CONV_W), CONV_K ** -0.5),
        'na_rpb': nrm(ks[14], (DEPTH, NA_HEADS, 2 * NA_KH - 1, 2 * NA_KW - 1), 0.1),
        'w_p_conv': nrm(ks[15], (DEPTH, CONV_W, D_MODEL), CONV_W ** -0.5),
        'w_p_mla': nrm(ks[16], (DEPTH, MLA_W, D_MODEL), MLA_W ** -0.5),
        'w_p_na': nrm(ks[17], (DEPTH, NA_W, D_MODEL), NA_W ** -0.5),
        'w_p_fnet': nrm(ks[18], (DEPTH, FNET_W, D_MODEL), FNET_W ** -0.5),
        'w_out': nrm(ks[19], (DEPTH, D_MODEL, D_MODEL), D_MODEL ** -0.5),
    }


def reference(x, c, ctx, c_ctx, g_pre, g_post, w_ada, b_ada, w_in, g_q, g_kv, w_uq, w_ukv,
              conv_w, na_rpb, w_p_conv, w_p_mla, w_p_na, w_p_fnet, w_out):
    xl, xc = x, ctx
    for i in range(DEPTH):
        xc, xl = _layer(xc, xl, c, c_ctx, g_pre[i], g_post[i], w_ada[i], b_ada[i], w_in[i],
                        g_q[i], g_kv[i], w_uq[i], w_ukv[i], conv_w[i], na_rpb[i],
                        w_p_conv[i], w_p_mla[i], w_p_na[i], w_p_fnet[i], w_out[i],
                        i < DEPTH - 1)
    return xl
```

```python
import functools

import jax
import jax.numpy as jnp
import numpy as np
from jax import lax
from jax.experimental import pallas as pl
from jax.experimental.pallas import tpu as pltpu

F32 = jnp.float32
BF16 = jnp.bfloat16

D_MODEL = 2048
DEPTH = 2
GRID_W = 64
CONV_W = 1024
CONV_K = 3
MLA_HEADS = 8
MLA_NOPE = 128
MLA_ROPE = 64
MLA_V = 128
Q_LORA = 512
KV_LORA = 256
MLA_W = MLA_HEADS * MLA_V
MLA_SCALE = (MLA_NOPE + MLA_ROPE) ** -0.5
NA_HEADS = 16
NA_HD = 64
NA_W = NA_HEADS * NA_HD
NA_KH = 8
NA_KW = 16
NA_SCALE = NA_HD ** -0.5
FNET_GROUPS = 4
FNET_GW = 256
FNET_W = FNET_GROUPS * FNET_GW
N_BRANCH = 4
ROPE_THETA = 10000.0
EPS = 1e-6
NEG = -1e30

VMEM_LIMIT_BYTES = 56 * 1024 * 1024

MLA_QK = 256
KV_SEG = 512
C_MG, C_G, C_CB, C_CC, C_CX, C_FV, C_NQ, C_NK, C_NV, C_QC, C_KV = (
    0, 8192, 12288, 13312, 14336, 15360, 16384, 17408, 18432, 19456, 19968)
NP_FULL = 20480
KV_BASE = C_NK


def _cparams(*sem):
    return pltpu.CompilerParams(dimension_semantics=sem, vmem_limit_bytes=VMEM_LIMIT_BYTES)


def _rms(x, g):
    return x * lax.rsqrt(jnp.mean(x * x, axis=-1, keepdims=True) + EPS) * g


def _silu(x):
    return x * jax.nn.sigmoid(x)


def _dot(a, b):
    return jnp.dot(a, b, preferred_element_type=F32)


def _dot_nt(a, b):
    return lax.dot_general(a, b, (((1,), (1,)), ((), ())), preferred_element_type=F32)


def _ada_kernel(c_ref, w_ref, b_ref, o_ref):
    a = _silu(c_ref[...]).astype(BF16)
    o_ref[0] = _dot(a, w_ref[0].astype(BF16)) + b_ref[0]


def _ada(cc, w_ada, b_ada):
    tn = 1024
    return pl.pallas_call(
        _ada_kernel,
        out_shape=jax.ShapeDtypeStruct((DEPTH, 8, 3 * D_MODEL), F32),
        grid=(DEPTH, 3 * D_MODEL // tn),
        in_specs=[pl.BlockSpec((8, D_MODEL), lambda l, j: (0, 0)),
                  pl.BlockSpec((1, D_MODEL, tn), lambda l, j: (l, 0, j)),
                  pl.BlockSpec((1, 1, tn), lambda l, j: (l, 0, j))],
        out_specs=pl.BlockSpec((1, 8, tn), lambda l, j: (l, 0, j)),
        compiler_params=_cparams("arbitrary", "arbitrary"),
        name="ada",
    )(cc, w_ada, b_ada)


def _inproj_kernel(x_ref, sh_ref, sc_ref, g_ref, w_ref, o_ref, u_ref):
    @pl.when(pl.program_id(1) == 0)
    def _():
        y = _rms(x_ref[...], g_ref[...])
        u_ref[...] = (y * (1.0 + sc_ref[0]) + sh_ref[0]).astype(BF16)

    o_ref[...] = _dot(u_ref[...], w_ref[...]).astype(o_ref.dtype)


def _inproj(x2, mod3, row_of_tile, g_pre, w, tm):
    t = x2.shape[0]
    n_p = w.shape[1]
    tn = 1024
    return pl.pallas_call(
        _inproj_kernel,
        out_shape=jax.ShapeDtypeStruct((t, n_p), BF16),
        grid=(t // tm, n_p // tn),
        in_specs=[pl.BlockSpec((tm, D_MODEL), lambda i, j: (i, 0)),
                  pl.BlockSpec((1, 1, D_MODEL), lambda i, j: (row_of_tile(i), 0, 0)),
                  pl.BlockSpec((1, 1, D_MODEL), lambda i, j: (row_of_tile(i), 0, 1)),
                  pl.BlockSpec((1, D_MODEL), lambda i, j: (0, 0)),
                  pl.BlockSpec((D_MODEL, tn), lambda i, j: (0, j))],
        out_specs=pl.BlockSpec((tm, tn), lambda i, j: (i, j)),
        scratch_shapes=[pltpu.VMEM((tm, D_MODEL), BF16)],
        compiler_params=_cparams("arbitrary", "arbitrary"),
        name="inproj",
    )(x2, mod3, mod3, g_pre, w)


def _qup_kernel(h_ref, g_ref, w_ref, cs_ref, o_ref):
    hn = _rms(h_ref[...].astype(F32), g_ref[...]).astype(BF16)
    qq = _dot(hn, w_ref[...]) * MLA_SCALE
    cs = cs_ref[...]
    for h in range(MLA_HEADS):
        o_ref[:, h * MLA_QK:h * MLA_QK + MLA_NOPE] = qq[:, h * 128:(h + 1) * 128].astype(BF16)
        t = qq[:, MLA_W + h * 128:MLA_W + (h + 1) * 128] * cs
        o_ref[:, h * MLA_QK + MLA_NOPE:(h + 1) * MLA_QK] = (t + pltpu.roll(t, 64, 1)).astype(BF16)


def _qup(p, col, g_q, w_q, cs, n_seq, tm):
    t = p.shape[0]
    spt = n_seq // tm
    return pl.pallas_call(
        _qup_kernel,
        out_shape=jax.ShapeDtypeStruct((t, MLA_HEADS * MLA_QK), BF16),
        grid=(t // tm,),
        in_specs=[pl.BlockSpec((tm, Q_LORA), lambda i: (i, col // Q_LORA)),
                  pl.BlockSpec((1, Q_LORA), lambda i: (0, 0)),
                  pl.BlockSpec((Q_LORA, 2 * MLA_W), lambda i: (0, 0)),
                  pl.BlockSpec((tm, 128), lambda i: (i % spt, 0))],
        out_specs=pl.BlockSpec((tm, MLA_HEADS * MLA_QK), lambda i: (i, 0)),
        compiler_params=_cparams("arbitrary"),
        name="mla_qup",
    )(p, g_q, w_q, cs)


def _kvup_kernel(p_ref, g_ref, w_ref, cs_ref, k_ref, v_ref):
    p = p_ref[...]
    cn = _rms(p[:, :KV_LORA].astype(F32), g_ref[...]).astype(BF16)
    kv = _dot(cn, w_ref[...])
    t = p[:, KV_LORA:KV_LORA + 128].astype(F32) * cs_ref[...]
    lane = lax.broadcasted_iota(jnp.int32, t.shape, 1)
    kr = jnp.where(lane < MLA_ROPE, t + pltpu.roll(t, 64, 1), 0.0).astype(BF16)
    for h in range(MLA_HEADS):
        k_ref[:, h * MLA_QK:h * MLA_QK + MLA_NOPE] = kv[:, h * 128:(h + 1) * 128].astype(BF16)
        k_ref[:, h * MLA_QK + MLA_NOPE:(h + 1) * MLA_QK] = kr
    v_ref[...] = kv[:, MLA_W:].astype(BF16)


def _kvup(p, col, g_kv, w_kv, cs, n_seq, tm):
    t = p.shape[0]
    spt = n_seq // tm
    return pl.pallas_call(
        _kvup_kernel,
        out_shape=(jax.ShapeDtypeStruct((t, MLA_HEADS * MLA_QK), BF16),
                   jax.ShapeDtypeStruct((t, MLA_W), BF16)),
        grid=(t // tm,),
        in_specs=[pl.BlockSpec((tm, KV_SEG), lambda i: (i, col // KV_SEG)),
                  pl.BlockSpec((1, KV_LORA), lambda i: (0, 0)),
                  pl.BlockSpec((KV_LORA, 2 * MLA_W), lambda i: (0, 0)),
                  pl.BlockSpec((tm, 128), lambda i: (i % spt, 0))],
        out_specs=(pl.BlockSpec((tm, MLA_HEADS * MLA_QK), lambda i: (i, 0)),
                   pl.BlockSpec((tm, MLA_W), lambda i: (i, 0))),
        compiler_params=_cparams("arbitrary"),
        name="mla_kvup",
    )(p, g_kv, w_kv, cs)


def _mla_attn_kernel(q_ref, kc_ref, vc_ref, *rest, has_lat):
    q = q_ref[...]
    s_c = _dot_nt(q, kc_ref[...])
    m = jnp.max(s_c, axis=-1, keepdims=True)
    if has_lat:
        kl_ref, vl_ref, o_ref = rest
        s_l = _dot_nt(q, kl_ref[...])
        m = jnp.maximum(m, jnp.max(s_l, axis=-1, keepdims=True))
        p_l = jnp.exp(s_l - m)
    else:
        (o_ref,) = rest
    p_c = jnp.exp(s_c - m)
    den = jnp.sum(p_c, axis=-1, keepdims=True)
    o = _dot(p_c.astype(BF16), vc_ref[...])
    if has_lat:
        den = den + jnp.sum(p_l, axis=-1, keepdims=True)
        o = o + _dot(p_l.astype(BF16), vl_ref[...])
    o_ref[...] = (o / den).astype(o_ref.dtype)


def _mla_attn(q, kc, vc, kl, vl, n_batch, n_q, n_ctx, tq):
    has_lat = kl is not None
    qt = n_q // tq
    in_specs = [pl.BlockSpec((tq, MLA_QK), lambda b, h, i: (b * qt + i, h)),
                pl.BlockSpec((n_ctx, MLA_QK), lambda b, h, i: (b, h)),
                pl.BlockSpec((n_ctx, MLA_V), lambda b, h, i: (b, h))]
    args = [q, kc, vc]
    if has_lat:
        in_specs += [pl.BlockSpec((n_q, MLA_QK), lambda b, h, i: (b, h)),
                     pl.BlockSpec((n_q, MLA_V), lambda b, h, i: (b, h))]
        args += [kl, vl]
    return pl.pallas_call(
        functools.partial(_mla_attn_kernel, has_lat=has_lat),
        out_shape=jax.ShapeDtypeStruct((n_batch * n_q, MLA_W), BF16),
        grid=(n_batch, MLA_HEADS, qt),
        in_specs=in_specs,
        out_specs=pl.BlockSpec((tq, MLA_V), lambda b, h, i: (b * qt + i, h)),
        compiler_params=_cparams("arbitrary", "arbitrary", "arbitrary"),
        name="mla_attn_lat" if has_lat else "mla_attn_ctx",
    )(*args)


def _pair_rows(q2):
    lane = lax.broadcasted_iota(jnp.int32, q2.shape, 1)
    zero = jnp.zeros_like(q2)
    return jnp.concatenate([jnp.where(lane < NA_HD, q2, zero), jnp.where(lane >= NA_HD, q2, zero)], axis=0)


def _unpair_rows(o, n):
    lane = lax.broadcasted_iota(jnp.int32, (n, 128), 1)
    return jnp.where(lane < NA_HD, o[:n], o[n:])


def _na_lat_kernel(q_ref, k_ref, v_ref, kc_ref, vc_ref, b_ref, o_ref):
    r = pl.program_id(1)
    rows = k_ref.shape[0] // GRID_W
    rs = jnp.clip(r - NA_KH // 2, 0, rows - NA_KH)
    start = pl.multiple_of(rs * GRID_W, GRID_W)
    for hp in range(NA_HEADS // 2):
        cols = slice(hp * 128, (hp + 1) * 128)
        qm = _pair_rows(q_ref[:, cols] * NA_SCALE)
        k2 = k_ref[pl.ds(start, NA_KH * GRID_W), cols]
        v2 = v_ref[pl.ds(start, NA_KH * GRID_W), cols]
        s_nb = _dot_nt(qm, k2) + b_ref[hp, 0]
        s_cx = _dot_nt(qm, kc_ref[:, cols])
        m = jnp.maximum(jnp.max(s_nb, axis=-1, keepdims=True), jnp.max(s_cx, axis=-1, keepdims=True))
        p_nb = jnp.exp(s_nb - m)
        p_cx = jnp.exp(s_cx - m)
        den = jnp.sum(p_nb, axis=-1, keepdims=True) + jnp.sum(p_cx, axis=-1, keepdims=True)
        o = (_dot(p_nb.astype(BF16), v2) + _dot(p_cx.astype(BF16), vc_ref[:, cols])) / den
        o_ref[:, cols] = _unpair_rows(o, GRID_W).astype(o_ref.dtype)


def _na_lat(p, pc, cq, ck, cv, cck, ccv, bias, n_batch, n_seq, n_ctx):
    rows = n_seq // GRID_W

    def rel(r):
        return r - jnp.clip(r - NA_KH // 2, 0, rows - NA_KH)

    return pl.pallas_call(
        _na_lat_kernel,
        out_shape=jax.ShapeDtypeStruct((n_batch * n_seq, NA_W), BF16),
        grid=(n_batch, rows),
        in_specs=[pl.BlockSpec((GRID_W, NA_W), lambda b, r: (b * rows + r, cq // NA_W)),
                  pl.BlockSpec((n_seq, NA_W), lambda b, r: (b, ck // NA_W)),
                  pl.BlockSpec((n_seq, NA_W), lambda b, r: (b, cv // NA_W)),
                  pl.BlockSpec((n_ctx, NA_W), lambda b, r: (b, cck // NA_W)),
                  pl.BlockSpec((n_ctx, NA_W), lambda b, r: (b, ccv // NA_W)),
                  pl.BlockSpec((NA_HEADS // 2, 1, 2 * GRID_W, NA_KH * GRID_W), lambda b, r: (0, rel(r), 0, 0))],
        out_specs=pl.BlockSpec((GRID_W, NA_W), lambda b, r: (b * rows + r, 0)),
        compiler_params=_cparams("arbitrary", "arbitrary"),
        name="na_lat",
    )(p, p, p, pc, pc, bias)


def _na_ctx_kernel(q_ref, kc_ref, vc_ref, o_ref):
    n = q_ref.shape[0]
    for hp in range(NA_HEADS // 2):
        cols = slice(hp * 128, (hp + 1) * 128)
        qm = _pair_rows(q_ref[:, cols] * NA_SCALE)
        s = _dot_nt(qm, kc_ref[:, cols])
        p = jnp.exp(s - jnp.max(s, axis=-1, keepdims=True))
        o = _dot(p.astype(BF16), vc_ref[:, cols]) / jnp.sum(p, axis=-1, keepdims=True)
        o_ref[:, cols] = _unpair_rows(o, n).astype(o_ref.dtype)


def _na_ctx(pc, cq, ck, cv, n_batch, n_ctx):
    return pl.pallas_call(
        _na_ctx_kernel,
        out_shape=jax.ShapeDtypeStruct((n_batch * n_ctx, NA_W), BF16),
        grid=(n_batch,),
        in_specs=[pl.BlockSpec((n_ctx, NA_W), lambda b: (b, cq // NA_W)),
                  pl.BlockSpec((n_ctx, NA_W), lambda b: (b, ck // NA_W)),
                  pl.BlockSpec((n_ctx, NA_W), lambda b: (b, cv // NA_W))],
        out_specs=pl.BlockSpec((n_ctx, NA_W), lambda b: (b, 0)),
        compiler_params=_cparams("arbitrary"),
        name="na_ctx",
    )(pc, pc, pc)


def _na_bias_table(rpb):
    rel = np.arange(NA_KH)[:, None]
    i = np.arange(NA_KH)[None, :]
    ro = i - rel + (NA_KH - 1)
    c = np.arange(GRID_W)[:, None]
    kc = np.arange(GRID_W)[None, :]
    cs = np.clip(c - NA_KW // 2, 0, GRID_W - NA_KW)
    valid = (kc >= cs) & (kc < cs + NA_KW)
    co = np.clip(kc - c + (NA_KW - 1), 0, 2 * NA_KW - 2)
    t = rpb.astype(F32)[:, ro[:, None, :, None], co[None, :, None, :]]
    t = jnp.where(valid[None, None, :, None, :], t, NEG)
    t = t.reshape(NA_HEADS // 2, 2, NA_KH, GRID_W, NA_KH * GRID_W)
    return t.transpose(0, 2, 1, 3, 4).reshape(NA_HEADS // 2, NA_KH, 2 * GRID_W, NA_KH * GRID_W)


def _conv_kernel(cb_ref, cc_ref, cx_ref, w_ref, o_ref):
    z = cc_ref[...].astype(F32) * cx_ref[...].astype(F32)
    n = z.shape[0]
    row = lax.broadcasted_iota(jnp.int32, z.shape, 0)
    zp = jnp.where(row == 0, 0.0, pltpu.roll(z, 1, 0))
    zn = jnp.where(row == n - 1, 0.0, pltpu.roll(z, n - 1, 0))
    w = w_ref[...]
    conv = zp * w[0:1] + z * w[1:2] + zn * w[2:3]
    o_ref[...] = (cb_ref[...].astype(F32) * conv).astype(o_ref.dtype)


def _conv(p, conv_w, n_batch, n_seq):
    tc = 256
    nb = CONV_W // tc
    return pl.pallas_call(
        _conv_kernel,
        out_shape=jax.ShapeDtypeStruct((n_batch * n_seq, CONV_W), BF16),
        grid=(n_batch, nb),
        in_specs=[pl.BlockSpec((n_seq, tc), lambda b, j: (b, C_CB // tc + j)),
                  pl.BlockSpec((n_seq, tc), lambda b, j: (b, C_CC // tc + j)),
                  pl.BlockSpec((n_seq, tc), lambda b, j: (b, C_CX // tc + j)),
                  pl.BlockSpec((CONV_K, tc), lambda b, j: (0, j))],
        out_specs=pl.BlockSpec((n_seq, tc), lambda b, j: (b, j)),
        compiler_params=_cparams("arbitrary", "arbitrary"),
        name="conv",
    )(p, p, p, conv_w)


def _fourier_kernel(x_ref, cg_ref, sg_ref, cs_ref, ss_ref, o_ref, *, norm):
    x = x_ref[...]
    a = _dot(x, cg_ref[...]).astype(BF16)
    b = _dot(x, sg_ref[...]).astype(BF16)
    o = _dot(cs_ref[...], a) - _dot(ss_ref[...], b)
    o_ref[...] = (o * norm).astype(o_ref.dtype)


def _dft_tables(n):
    j = jnp.arange(n, dtype=jnp.int32)
    jk = (j[:, None] * j[None, :]) % n
    ang = jk.astype(F32) * (2.0 * np.pi / n)
    return jnp.cos(ang).astype(BF16), jnp.sin(ang).astype(BF16)


def _fourier(p, n_batch, n_seq):
    cg, sg = _dft_tables(FNET_GW)
    cs, ss = _dft_tables(n_seq)
    const = lambda b, g: (0, 0)
    return pl.pallas_call(
        functools.partial(_fourier_kernel, norm=float((n_seq * FNET_GW) ** -0.5)),
        out_shape=jax.ShapeDtypeStruct((n_batch * n_seq, FNET_W), BF16),
        grid=(n_batch, FNET_GROUPS),
        in_specs=[pl.BlockSpec((n_seq, FNET_GW), lambda b, g: (b, C_FV // FNET_GW + g)),
                  pl.BlockSpec((FNET_GW, FNET_GW), const),
                  pl.BlockSpec((FNET_GW, FNET_GW), const),
                  pl.BlockSpec((n_seq, n_seq), const, pipeline_mode=pl.Buffered(1)),
                  pl.BlockSpec((n_seq, n_seq), const, pipeline_mode=pl.Buffered(1))],
        out_specs=pl.BlockSpec((n_seq, FNET_GW), lambda b, g: (b, g)),
        compiler_params=_cparams("arbitrary", "arbitrary"),
        name="fourier",
    )(p, cg, sg, cs, ss)


def _combine_kernel(o0_ref, o1_ref, o2_ref, o3_ref, g_ref, mg_ref, w_ref, m_ref, acc_ref):
    k = pl.program_id(1)

    def branch(o_ref, first):
        t = (o_ref[...].astype(F32) * _silu(g_ref[...].astype(F32))).astype(BF16)
        r = _dot(t, w_ref[0]) * jax.nn.sigmoid(mg_ref[...].astype(F32))
        if first:
            acc_ref[...] = r
        else:
            acc_ref[...] += r

    for kk, o_ref in enumerate((o0_ref, o1_ref, o2_ref, o3_ref)):
        pl.when(k == kk)(functools.partial(branch, o_ref, kk == 0))

    @pl.when(k == N_BRANCH - 1)
    def _():
        m_ref[...] = acc_ref[...].astype(m_ref.dtype)


def _combine(p, outs, w_p, tm):
    t = p.shape[0]
    o_spec = pl.BlockSpec((tm, 1024), lambda i, k: (i, 0))
    return pl.pallas_call(
        _combine_kernel,
        out_shape=jax.ShapeDtypeStruct((t, D_MODEL), BF16),
        grid=(t // tm, N_BRANCH),
        in_specs=[o_spec, o_spec, o_spec, o_spec,
                  pl.BlockSpec((tm, 1024), lambda i, k: (i, C_G // 1024 + k)),
                  pl.BlockSpec((tm, D_MODEL), lambda i, k: (i, C_MG // D_MODEL + k)),
                  pl.BlockSpec((1, 1024, D_MODEL), lambda i, k: (k, 0, 0))],
        out_specs=pl.BlockSpec((tm, D_MODEL), lambda i, k: (i, 0)),
        scratch_shapes=[pltpu.VMEM((tm, D_MODEL), F32)],
        compiler_params=_cparams("arbitrary", "arbitrary"),
        name="combine",
    )(*outs, p, p, w_p)


def _outproj_kernel(m_ref, w_ref, x_ref, gt_ref, g_ref, o_ref):
    y = _dot(m_ref[...], w_ref[...])
    o_ref[...] = x_ref[...] + gt_ref[0] * _rms(y, g_ref[...])


def _outproj(m, w_out, x2, mod3, row_of_tile, g_post, tm):
    t = m.shape[0]
    return pl.pallas_call(
        _outproj_kernel,
        out_shape=jax.ShapeDtypeStruct((t, D_MODEL), F32),
        grid=(t // tm,),
        in_specs=[pl.BlockSpec((tm, D_MODEL), lambda i: (i, 0)),
                  pl.BlockSpec((D_MODEL, D_MODEL), lambda i: (0, 0), pipeline_mode=pl.Buffered(1)),
                  pl.BlockSpec((tm, D_MODEL), lambda i: (i, 0)),
                  pl.BlockSpec((1, 1, D_MODEL), lambda i: (row_of_tile(i), 0, 2)),
                  pl.BlockSpec((1, D_MODEL), lambda i: (0, 0))],
        out_specs=pl.BlockSpec((tm, D_MODEL), lambda i: (i, 0)),
        compiler_params=_cparams("arbitrary"),
        name="outproj",
    )(m, w_out, x2, mod3, g_post)


def _rope_swap(w):
    return jnp.concatenate([-w[..., 16:32], w[..., 0:16], -w[..., 48:64], w[..., 32:48]], axis=-1)


def _prep_w_in(w):
    kr = w[:, KV_LORA:KV_LORA + MLA_ROPE]
    o_nk, o_qc, o_nq, o_cb, o_fv, o_g, o_mg = 320, 2368, 2880, 3904, 6976, 8000, 12096
    parts = [w[:, o_mg:o_mg + N_BRANCH * D_MODEL], w[:, o_g:o_g + 4096], w[:, o_cb:o_cb + 3 * CONV_W],
             w[:, o_fv:o_fv + FNET_W], w[:, o_nq:o_nq + NA_W], w[:, o_nk:o_nk + 2 * NA_W],
             w[:, o_qc:o_qc + Q_LORA], w[:, :KV_LORA + MLA_ROPE], _rope_swap(kr),
             jnp.zeros((D_MODEL, 128), w.dtype)]
    return jnp.concatenate(parts, axis=1).astype(BF16)


def _prep_w_uq(w):
    w3 = w.reshape(Q_LORA, MLA_HEADS, MLA_NOPE + MLA_ROPE)
    nope = w3[:, :, :MLA_NOPE].reshape(Q_LORA, MLA_W)
    r = w3[:, :, MLA_NOPE:]
    rr = jnp.concatenate([r, _rope_swap(r)], axis=-1).reshape(Q_LORA, MLA_HEADS * 128)
    return jnp.concatenate([nope, rr], axis=1).astype(BF16)


def _prep_w_ukv(w):
    w3 = w.reshape(KV_LORA, MLA_HEADS, MLA_NOPE + MLA_V)
    return jnp.concatenate([w3[:, :, :MLA_NOPE].reshape(KV_LORA, MLA_W),
                            w3[:, :, MLA_NOPE:].reshape(KV_LORA, MLA_W)], axis=1).astype(BF16)


def _rope_table(n):
    nf = MLA_ROPE // 4
    t = jnp.arange(n, dtype=jnp.int32)
    pos = jnp.stack([t // GRID_W, t % GRID_W], axis=-1).astype(F32)
    inv = ROPE_THETA ** (-jnp.arange(nf, dtype=F32) / nf)
    ang = pos[:, :, None] * inv
    cos = jnp.repeat(jnp.cos(ang), 2, axis=1).reshape(n, MLA_ROPE)
    sin = jnp.repeat(jnp.sin(ang), 2, axis=1).reshape(n, MLA_ROPE)
    return jnp.concatenate([cos, sin], axis=-1).astype(F32)


def _identity_rope_table(n):
    return jnp.concatenate([jnp.ones((n, MLA_ROPE), F32), jnp.zeros((n, MLA_ROPE), F32)], axis=-1)


def kernel(x, c, ctx, c_ctx, g_pre, g_post, w_ada, b_ada, w_in, g_q, g_kv, w_uq, w_ukv, conv_w, na_rpb,
           w_p_conv, w_p_mla, w_p_na, w_p_fnet, w_out):
    n_batch, n_seq, _ = x.shape
    n_ctx = ctx.shape[1]
    assert n_batch + 1 <= 8 and n_seq % 1024 == 0 and n_ctx % 256 == 0
    xl = x.reshape(n_batch * n_seq, D_MODEL)
    xc = ctx.reshape(n_batch * n_ctx, D_MODEL)

    cc = jnp.concatenate([c, c_ctx[None], jnp.zeros((8 - n_batch - 1, D_MODEL), c.dtype)], axis=0)
    mod = _ada(cc, w_ada, b_ada.reshape(DEPTH, 1, 3 * D_MODEL))

    rope_l = _rope_table(n_seq)
    rope_c = _identity_rope_table(n_ctx)
    tm_l = 1024
    tm_c = n_batch * n_ctx
    lat_row = lambda i: i * tm_l // n_seq
    ctx_row = lambda i: n_batch

    def lat_row_t(tm):
        return lambda i: i * tm // n_seq

    for l in range(DEPTH):
        ctx_out = l < DEPTH - 1
        mod3 = mod[l].reshape(8, 1, 3 * D_MODEL)
        w_in_p = _prep_w_in(w_in[l])
        w_q = _prep_w_uq(w_uq[l])
        w_kv = _prep_w_ukv(w_ukv[l])
        gq = g_q[l].reshape(1, Q_LORA)
        gkv = g_kv[l].reshape(1, KV_LORA)
        gpre = g_pre[l].reshape(1, D_MODEL)
        gpost = g_post[l].reshape(1, D_MODEL)
        w_p = jnp.stack([w_p_conv[l], w_p_mla[l], w_p_na[l], w_p_fnet[l]]).astype(BF16)
        w_o = w_out[l].astype(BF16)
        bias = _na_bias_table(na_rpb[l])

        pl_ = _inproj(xl, mod3, lat_row, gpre, w_in_p, tm_l)
        if ctx_out:
            pc = _inproj(xc, mod3, ctx_row, gpre, w_in_p, tm_c)
            base = 0
        else:
            pc = _inproj(xc, mod3, ctx_row, gpre, w_in_p[:, KV_BASE:], tm_c)
            base = KV_BASE

        kc_m, vc_m = _kvup(pc, C_KV - base, gkv, w_kv, rope_c, n_ctx, n_ctx)
        kl_m, vl_m = _kvup(pl_, C_KV, gkv, w_kv, rope_l, n_seq, 512)
        ql_m = _qup(pl_, C_QC, gq, w_q, rope_l, n_seq, 512)
        mla_l = _mla_attn(ql_m, kc_m, vc_m, kl_m, vl_m, n_batch, n_seq, n_ctx, 512)
        na_l = _na_lat(pl_, pc, C_NQ, C_NK, C_NV, C_NK - base, C_NV - base, bias, n_batch, n_seq, n_ctx)
        conv_l = _conv(pl_, conv_w[l], n_batch, n_seq)
        fn_l = _fourier(pl_, n_batch, n_seq)
        m_l = _combine(pl_, (conv_l, mla_l, na_l, fn_l), w_p, 512)
        xl_new = _outproj(m_l, w_o, xl, mod3, lat_row_t(512), gpost, 512)

        if ctx_out:
            qc_m = _qup(pc, C_QC, gq, w_q, rope_c, n_ctx, n_ctx)
            mla_c = _mla_attn(qc_m, kc_m, vc_m, None, None, n_batch, n_ctx, n_ctx, n_ctx)
            na_c = _na_ctx(pc, C_NQ, C_NK, C_NV, n_batch, n_ctx)
            conv_c = _conv(pc, conv_w[l], n_batch, n_ctx)
            fn_c = _fourier(pc, n_batch, n_ctx)
            m_c = _combine(pc, (conv_c, mla_c, na_c, fn_c), w_p, min(512, tm_c))
            xc = _outproj(m_c, w_o, xc, mod3, ctx_row, gpost, min(512, tm_c))
        xl = xl_new

    return xl.reshape(n_batch, n_seq, D_MODEL)
```

```python
import functools

import jax
import jax.numpy as jnp
import numpy as np
from jax import lax
from jax.experimental import pallas as pl
from jax.experimental.pallas import tpu as pltpu

F32 = jnp.float32
BF16 = jnp.bfloat16

D_MODEL = 2048
DEPTH = 2
GRID_W = 64
CONV_W = 1024
CONV_K = 3
MLA_HEADS = 8
MLA_NOPE = 128
MLA_ROPE = 64
MLA_V = 128
Q_LORA = 512
KV_LORA = 256
MLA_W = MLA_HEADS * MLA_V
MLA_SCALE = (MLA_NOPE + MLA_ROPE) ** -0.5
NA_HEADS = 16
NA_HD = 64
NA_W = NA_HEADS * NA_HD
NA_KH = 8
NA_KW = 16
NA_SCALE = NA_HD ** -0.5
FNET_GROUPS = 4
FNET_GW = 256
FNET_W = FNET_GROUPS * FNET_GW
N_BRANCH = 4
ROPE_THETA = 10000.0
EPS = 1e-6
NEG = -1e30

VMEM_LIMIT_BYTES = 56 * 1024 * 1024

MLA_QK = 256
KV_SEG = 512
C_MG, C_G, C_CB, C_CC, C_CX, C_FV, C_NQ, C_NK, C_NV, C_QC, C_KV = (
    0, 8192, 12288, 13312, 14336, 15360, 16384, 17408, 18432, 19456, 19968)
NP_FULL = 20480
KV_BASE = C_NK


def _cparams(*sem):
    return pltpu.CompilerParams(dimension_semantics=sem, vmem_limit_bytes=VMEM_LIMIT_BYTES)


def _rms(x, g):
    return x * lax.rsqrt(jnp.mean(x * x, axis=-1, keepdims=True) + EPS) * g


def _silu(x):
    return x * jax.nn.sigmoid(x)


def _dot(a, b):
    return jnp.dot(a, b, preferred_element_type=F32)


def _dot_nt(a, b):
    return lax.dot_general(a, b, (((1,), (1,)), ((), ())), preferred_element_type=F32)


def _ada_kernel(c_ref, w_ref, b_ref, o_ref):
    a = _silu(c_ref[...]).astype(BF16)
    o_ref[0] = _dot(a, w_ref[0].astype(BF16)) + b_ref[0]


def _ada(cc, w_ada, b_ada):
    tn = 1024
    return pl.pallas_call(
        _ada_kernel,
        out_shape=jax.ShapeDtypeStruct((DEPTH, 8, 3 * D_MODEL), F32),
        grid=(DEPTH, 3 * D_MODEL // tn),
        in_specs=[pl.BlockSpec((8, D_MODEL), lambda l, j: (0, 0)),
                  pl.BlockSpec((1, D_MODEL, tn), lambda l, j: (l, 0, j)),
                  pl.BlockSpec((1, 1, tn), lambda l, j: (l, 0, j))],
        out_specs=pl.BlockSpec((1, 8, tn), lambda l, j: (l, 0, j)),
        compiler_params=_cparams("arbitrary", "arbitrary"),
        name="ada",
    )(cc, w_ada, b_ada)


def _prenorm_kernel(x_ref, sh_ref, sc_ref, g_ref, u_ref):
    y = _rms(x_ref[...], g_ref[...])
    u_ref[...] = (y * (1.0 + sc_ref[0]) + sh_ref[0]).astype(BF16)


def _prenorm(x2, mod3, row_of_tile, g_pre, tm):
    t = x2.shape[0]
    return pl.pallas_call(
        _prenorm_kernel,
        out_shape=jax.ShapeDtypeStruct((t, D_MODEL), BF16),
        grid=(t // tm,),
        in_specs=[pl.BlockSpec((tm, D_MODEL), lambda i: (i, 0)),
                  pl.BlockSpec((1, 1, D_MODEL), lambda i: (row_of_tile(i), 0, 0)),
                  pl.BlockSpec((1, 1, D_MODEL), lambda i: (row_of_tile(i), 0, 1)),
                  pl.BlockSpec((1, D_MODEL), lambda i: (0, 0))],
        out_specs=pl.BlockSpec((tm, D_MODEL), lambda i: (i, 0)),
        compiler_params=_cparams("arbitrary"),
        name="prenorm",
    )(x2, mod3, mod3, g_pre)


TN = 1024
W_ROWS = 256
N_ORIG = 20288

_TILE_SRC = ([12096 + TN * t for t in range(8)] + [8000 + TN * t for t in range(4)]
             + [3904 + TN * t for t in range(3)] + [6976, 2880, 320, 1344, -1])


def _inproj_kernel(src_ref, nxt_ref, u_ref, wt_hbm, tail_ref, o_ref, stage, wbuf, sem, *, layer, first_w):
    j = pl.program_id(0)

    def copy(jj):
        start = pl.multiple_of(src_ref[jj], 8)
        return pltpu.make_async_copy(wt_hbm.at[layer, pl.ds(start, TN), :], stage, sem)

    @pl.when(pl.program_id(1) == 0)
    def _():
        if first_w is not None:
            @pl.when(j == 0)
            def _():
                copy(first_w).start()

        @pl.when(src_ref[j] >= 0)
        def _():
            copy(j).wait()

            def body(r, carry):
                r0 = pl.multiple_of(r * W_ROWS, W_ROWS)
                wbuf[pl.ds(r0, W_ROWS), :] = stage[pl.ds(r0, W_ROWS), :].astype(BF16)
                return carry

            lax.fori_loop(0, TN // W_ROWS, body, 0)

            @pl.when(nxt_ref[j] >= 0)
            def _():
                copy(nxt_ref[j]).start()

        @pl.when(src_ref[j] < 0)
        def _():
            wbuf[...] = tail_ref[...]

    o_ref[...] = _dot_nt(u_ref[...], wbuf[...]).astype(o_ref.dtype)


def _inproj(u, wt_in, layer, w_tail, tiles, tm):
    t = u.shape[0]
    src = np.array([_TILE_SRC[j] for j in tiles], np.int32)
    assert all(s % 8 == 0 and s + TN <= N_ORIG for s in src[src >= 0])
    is_w = [bool(s >= 0) for s in src]
    nxt = np.full(len(tiles), -1, np.int32)
    for a in range(len(tiles)):
        later = [b for b in range(a + 1, len(tiles)) if is_w[b]]
        if is_w[a] and later:
            nxt[a] = later[0]
    first_w = is_w.index(True) if any(is_w) else None
    grid_spec = pltpu.PrefetchScalarGridSpec(
        num_scalar_prefetch=2,
        grid=(len(tiles), t // tm),
        in_specs=[pl.BlockSpec((tm, D_MODEL), lambda j, i, s, n: (i, 0)),
                  pl.BlockSpec(memory_space=pl.ANY),
                  pl.BlockSpec((TN, D_MODEL), lambda j, i, s, n: (0, 0))],
        out_specs=pl.BlockSpec((tm, TN), lambda j, i, s, n: (i, j)),
        scratch_shapes=[pltpu.VMEM((TN, D_MODEL), F32), pltpu.VMEM((TN, D_MODEL), BF16),
                        pltpu.SemaphoreType.DMA(())])
    return pl.pallas_call(
        functools.partial(_inproj_kernel, layer=layer, first_w=first_w),
        out_shape=jax.ShapeDtypeStruct((t, len(tiles) * TN), BF16),
        grid_spec=grid_spec,
        compiler_params=_cparams("arbitrary", "arbitrary"),
        name="inproj",
    )(jnp.asarray(src), jnp.asarray(nxt), u, wt_in, w_tail)


def _qup_kernel(h_ref, g_ref, w_ref, cs_ref, o_ref):
    hn = _rms(h_ref[...].astype(F32), g_ref[...]).astype(BF16)
    qq = _dot(hn, w_ref[...]) * MLA_SCALE
    cs = cs_ref[...]
    for h in range(MLA_HEADS):
        o_ref[:, h * MLA_QK:h * MLA_QK + MLA_NOPE] = qq[:, h * 128:(h + 1) * 128].astype(BF16)
        t = qq[:, MLA_W + h * 128:MLA_W + (h + 1) * 128] * cs
        o_ref[:, h * MLA_QK + MLA_NOPE:(h + 1) * MLA_QK] = (t + pltpu.roll(t, 64, 1)).astype(BF16)


def _qup(p, col, g_q, w_q, cs, n_seq, tm):
    t = p.shape[0]
    spt = n_seq // tm
    return pl.pallas_call(
        _qup_kernel,
        out_shape=jax.ShapeDtypeStruct((t, MLA_HEADS * MLA_QK), BF16),
        grid=(t // tm,),
        in_specs=[pl.BlockSpec((tm, Q_LORA), lambda i: (i, col // Q_LORA)),
                  pl.BlockSpec((1, Q_LORA), lambda i: (0, 0)),
                  pl.BlockSpec((Q_LORA, 2 * MLA_W), lambda i: (0, 0)),
                  pl.BlockSpec((tm, 128), lambda i: (i % spt, 0))],
        out_specs=pl.BlockSpec((tm, MLA_HEADS * MLA_QK), lambda i: (i, 0)),
        compiler_params=_cparams("arbitrary"),
        name="mla_qup",
    )(p, g_q, w_q, cs)


def _kvup_kernel(p_ref, g_ref, w_ref, cs_ref, k_ref, v_ref):
    p = p_ref[...]
    cn = _rms(p[:, :KV_LORA].astype(F32), g_ref[...]).astype(BF16)
    kv = _dot(cn, w_ref[...])
    t = p[:, KV_LORA:KV_LORA + 128].astype(F32) * cs_ref[...]
    lane = lax.broadcasted_iota(jnp.int32, t.shape, 1)
    kr = jnp.where(lane < MLA_ROPE, t + pltpu.roll(t, 64, 1), 0.0).astype(BF16)
    for h in range(MLA_HEADS):
        k_ref[:, h * MLA_QK:h * MLA_QK + MLA_NOPE] = kv[:, h * 128:(h + 1) * 128].astype(BF16)
        k_ref[:, h * MLA_QK + MLA_NOPE:(h + 1) * MLA_QK] = kr
    v_ref[...] = kv[:, MLA_W:].astype(BF16)


def _kvup(p, col, g_kv, w_kv, cs, n_seq, tm):
    t = p.shape[0]
    spt = n_seq // tm
    return pl.pallas_call(
        _kvup_kernel,
        out_shape=(jax.ShapeDtypeStruct((t, MLA_HEADS * MLA_QK), BF16),
                   jax.ShapeDtypeStruct((t, MLA_W), BF16)),
        grid=(t // tm,),
        in_specs=[pl.BlockSpec((tm, KV_SEG), lambda i: (i, col // KV_SEG)),
                  pl.BlockSpec((1, KV_LORA), lambda i: (0, 0)),
                  pl.BlockSpec((KV_LORA, 2 * MLA_W), lambda i: (0, 0)),
                  pl.BlockSpec((tm, 128), lambda i: (i % spt, 0))],
        out_specs=(pl.BlockSpec((tm, MLA_HEADS * MLA_QK), lambda i: (i, 0)),
                   pl.BlockSpec((tm, MLA_W), lambda i: (i, 0))),
        compiler_params=_cparams("arbitrary"),
        name="mla_kvup",
    )(p, g_kv, w_kv, cs)


MLA_ROWS = 256


def _mla_attn_kernel(q_ref, kc_ref, vc_ref, *rest, has_lat):
    if has_lat:
        kl_ref, vl_ref, o_ref = rest
    else:
        (o_ref,) = rest
    for r0 in range(0, q_ref.shape[0], MLA_ROWS):
        rows = slice(r0, r0 + MLA_ROWS)
        q = q_ref[rows, :]
        s_c = _dot_nt(q, kc_ref[...])
        m = jnp.max(s_c, axis=-1, keepdims=True)
        if has_lat:
            s_l = _dot_nt(q, kl_ref[...])
            m = jnp.maximum(m, jnp.max(s_l, axis=-1, keepdims=True))
            p_l = jnp.exp(s_l - m)
        p_c = jnp.exp(s_c - m)
        den = jnp.sum(p_c, axis=-1, keepdims=True)
        o = _dot(p_c.astype(BF16), vc_ref[...])
        if has_lat:
            den = den + jnp.sum(p_l, axis=-1, keepdims=True)
            o = o + _dot(p_l.astype(BF16), vl_ref[...])
        o_ref[rows, :] = (o / den).astype(o_ref.dtype)


def _mla_attn(q, kc, vc, kl, vl, n_batch, n_q, n_ctx, tq):
    has_lat = kl is not None
    qt = n_q // tq
    in_specs = [pl.BlockSpec((tq, MLA_QK), lambda b, h, i: (b * qt + i, h)),
                pl.BlockSpec((n_ctx, MLA_QK), lambda b, h, i: (b, h)),
                pl.BlockSpec((n_ctx, MLA_V), lambda b, h, i: (b, h))]
    args = [q, kc, vc]
    if has_lat:
        in_specs += [pl.BlockSpec((n_q, MLA_QK), lambda b, h, i: (b, h)),
                     pl.BlockSpec((n_q, MLA_V), lambda b, h, i: (b, h))]
        args += [kl, vl]
    return pl.pallas_call(
        functools.partial(_mla_attn_kernel, has_lat=has_lat),
        out_shape=jax.ShapeDtypeStruct((n_batch * n_q, MLA_W), BF16),
        grid=(n_batch, MLA_HEADS, qt),
        in_specs=in_specs,
        out_specs=pl.BlockSpec((tq, MLA_V), lambda b, h, i: (b * qt + i, h)),
        compiler_params=_cparams("arbitrary", "arbitrary", "arbitrary"),
        name="mla_attn_lat" if has_lat else "mla_attn_ctx",
    )(*args)


def _pair_rows(q2):
    lane = lax.broadcasted_iota(jnp.int32, q2.shape, 1)
    zero = jnp.zeros_like(q2)
    return jnp.concatenate([jnp.where(lane < NA_HD, q2, zero), jnp.where(lane >= NA_HD, q2, zero)], axis=0)


def _unpair_rows(o, n):
    lane = lax.broadcasted_iota(jnp.int32, (n, 128), 1)
    return jnp.where(lane < NA_HD, o[:n], o[n:])


def _na_lat_kernel(q_ref, k_ref, v_ref, kc_ref, vc_ref, b_ref, o_ref):
    r = pl.program_id(1)
    rows = k_ref.shape[0] // GRID_W
    rs = jnp.clip(r - NA_KH // 2, 0, rows - NA_KH)
    rel = r - rs
    start = pl.multiple_of(rs * GRID_W, GRID_W)
    for hp in range(NA_HEADS // 2):
        cols = slice(hp * 128, (hp + 1) * 128)
        qm = _pair_rows(q_ref[:, cols] * NA_SCALE)
        k2 = k_ref[pl.ds(start, NA_KH * GRID_W), cols]
        v2 = v_ref[pl.ds(start, NA_KH * GRID_W), cols]
        bias = jnp.concatenate([b_ref[hp, NA_KH - 1 - rel + i] for i in range(0, NA_KH, 2)], axis=1)
        s_nb = _dot_nt(qm, k2) + bias
        s_cx = _dot_nt(qm, kc_ref[:, cols])
        m = jnp.maximum(jnp.max(s_nb, axis=-1, keepdims=True), jnp.max(s_cx, axis=-1, keepdims=True))
        p_nb = jnp.exp(s_nb - m)
        p_cx = jnp.exp(s_cx - m)
        den = jnp.sum(p_nb, axis=-1, keepdims=True) + jnp.sum(p_cx, axis=-1, keepdims=True)
        o = (_dot(p_nb.astype(BF16), v2) + _dot(p_cx.astype(BF16), vc_ref[:, cols])) / den
        o_ref[:, cols] = _unpair_rows(o, GRID_W).astype(o_ref.dtype)


def _na_lat(p, pc, cq, ck, cv, cck, ccv, bias, n_batch, n_seq, n_ctx):
    rows = n_seq // GRID_W
    return pl.pallas_call(
        _na_lat_kernel,
        out_shape=jax.ShapeDtypeStruct((n_batch * n_seq, NA_W), BF16),
        grid=(n_batch, rows),
        in_specs=[pl.BlockSpec((GRID_W, NA_W), lambda b, r: (b * rows + r, cq // NA_W)),
                  pl.BlockSpec((n_seq, NA_W), lambda b, r: (b, ck // NA_W)),
                  pl.BlockSpec((n_seq, NA_W), lambda b, r: (b, cv // NA_W)),
                  pl.BlockSpec((n_ctx, NA_W), lambda b, r: (b, cck // NA_W)),
                  pl.BlockSpec((n_ctx, NA_W), lambda b, r: (b, ccv // NA_W)),
                  pl.BlockSpec((NA_HEADS // 2, 2 * NA_KH - 2, 2 * GRID_W, 2 * GRID_W), lambda b, r: (0, 0, 0, 0),
                               pipeline_mode=pl.Buffered(1))],
        out_specs=pl.BlockSpec((GRID_W, NA_W), lambda b, r: (b * rows + r, 0)),
        compiler_params=_cparams("arbitrary", "arbitrary"),
        name="na_lat",
    )(p, p, p, pc, pc, bias)


def _na_ctx_kernel(q_ref, kc_ref, vc_ref, o_ref):
    n = q_ref.shape[0]
    for hp in range(NA_HEADS // 2):
        cols = slice(hp * 128, (hp + 1) * 128)
        qm = _pair_rows(q_ref[:, cols] * NA_SCALE)
        s = _dot_nt(qm, kc_ref[:, cols])
        p = jnp.exp(s - jnp.max(s, axis=-1, keepdims=True))
        o = _dot(p.astype(BF16), vc_ref[:, cols]) / jnp.sum(p, axis=-1, keepdims=True)
        o_ref[:, cols] = _unpair_rows(o, n).astype(o_ref.dtype)


def _na_ctx(pc, cq, ck, cv, n_batch, n_ctx):
    return pl.pallas_call(
        _na_ctx_kernel,
        out_shape=jax.ShapeDtypeStruct((n_batch * n_ctx, NA_W), BF16),
        grid=(n_batch,),
        in_specs=[pl.BlockSpec((n_ctx, NA_W), lambda b: (b, cq // NA_W)),
                  pl.BlockSpec((n_ctx, NA_W), lambda b: (b, ck // NA_W)),
                  pl.BlockSpec((n_ctx, NA_W), lambda b: (b, cv // NA_W))],
        out_specs=pl.BlockSpec((n_ctx, NA_W), lambda b: (b, 0)),
        compiler_params=_cparams("arbitrary"),
        name="na_ctx",
    )(pc, pc, pc)


def _na_bias_table(rpb):
    c = np.arange(GRID_W)[:, None]
    kc = np.arange(GRID_W)[None, :]
    cs = np.clip(c - NA_KW // 2, 0, GRID_W - NA_KW)
    valid = (kc >= cs) & (kc < cs + NA_KW)
    co = kc - c + (NA_KW - 1)
    onehot = (valid[None] & (co[None] == np.arange(2 * NA_KW - 1)[:, None, None])).astype(np.float32)
    t = jnp.einsum('hrj,jck->hrck', rpb.astype(F32), onehot, precision=lax.Precision.HIGHEST)
    t = t + np.where(valid, 0.0, NEG).astype(np.float32)
    t = t.reshape(NA_HEADS // 2, 2, 2 * NA_KH - 1, GRID_W, GRID_W).transpose(0, 2, 1, 3, 4)
    t = t.reshape(NA_HEADS // 2, 2 * NA_KH - 1, 2 * GRID_W, GRID_W)
    return jnp.concatenate([t[:, :-1], t[:, 1:]], axis=-1)


def _conv_kernel(cb_ref, cc_ref, cx_ref, w_ref, o_ref):
    z = cc_ref[...].astype(F32) * cx_ref[...].astype(F32)
    n = z.shape[0]
    row = lax.broadcasted_iota(jnp.int32, z.shape, 0)
    zp = jnp.where(row == 0, 0.0, pltpu.roll(z, 1, 0))
    zn = jnp.where(row == n - 1, 0.0, pltpu.roll(z, n - 1, 0))
    w = w_ref[...]
    conv = zp * w[0:1] + z * w[1:2] + zn * w[2:3]
    o_ref[...] = (cb_ref[...].astype(F32) * conv).astype(o_ref.dtype)


def _conv(p, conv_w, n_batch, n_seq):
    tc = 256
    nb = CONV_W // tc
    return pl.pallas_call(
        _conv_kernel,
        out_shape=jax.ShapeDtypeStruct((n_batch * n_seq, CONV_W), BF16),
        grid=(n_batch, nb),
        in_specs=[pl.BlockSpec((n_seq, tc), lambda b, j: (b, C_CB // tc + j)),
                  pl.BlockSpec((n_seq, tc), lambda b, j: (b, C_CC // tc + j)),
                  pl.BlockSpec((n_seq, tc), lambda b, j: (b, C_CX // tc + j)),
                  pl.BlockSpec((CONV_K, tc), lambda b, j: (0, j))],
        out_specs=pl.BlockSpec((n_seq, tc), lambda b, j: (b, j)),
        compiler_params=_cparams("arbitrary", "arbitrary"),
        name="conv",
    )(p, p, p, conv_w)


def _fourier_kernel(x_ref, cg_ref, sg_ref, cs_ref, ss_ref, o_ref, *, norm):
    x = x_ref[...]
    a = _dot(x, cg_ref[...]).astype(BF16)
    b = _dot(x, sg_ref[...]).astype(BF16)
    o = _dot(cs_ref[...], a) - _dot(ss_ref[...], b)
    o_ref[...] = (o * norm).astype(o_ref.dtype)


@functools.lru_cache(maxsize=None)
def _dft_tables(n):
    j = np.arange(n, dtype=np.int64)
    ang = ((j[:, None] * j[None, :]) % n).astype(np.float64) * (2.0 * np.pi / n)
    return np.cos(ang).astype(np.float32), np.sin(ang).astype(np.float32)


def _fourier(p, n_batch, n_seq):
    cg, sg = (jnp.asarray(t).astype(BF16) for t in _dft_tables(FNET_GW))
    cs, ss = (jnp.asarray(t).astype(BF16) for t in _dft_tables(n_seq))
    const = lambda b, g: (0, 0)
    return pl.pallas_call(
        functools.partial(_fourier_kernel, norm=float((n_seq * FNET_GW) ** -0.5)),
        out_shape=jax.ShapeDtypeStruct((n_batch * n_seq, FNET_W), BF16),
        grid=(n_batch, FNET_GROUPS),
        in_specs=[pl.BlockSpec((n_seq, FNET_GW), lambda b, g: (b, C_FV // FNET_GW + g)),
                  pl.BlockSpec((FNET_GW, FNET_GW), const),
                  pl.BlockSpec((FNET_GW, FNET_GW), const),
                  pl.BlockSpec((n_seq, n_seq), const, pipeline_mode=pl.Buffered(1)),
                  pl.BlockSpec((n_seq, n_seq), const, pipeline_mode=pl.Buffered(1))],
        out_specs=pl.BlockSpec((n_seq, FNET_GW), lambda b, g: (b, g)),
        compiler_params=_cparams("arbitrary", "arbitrary"),
        name="fourier",
    )(p, cg, sg, cs, ss)


def _combine_kernel(o0_ref, o1_ref, o2_ref, o3_ref, g_ref, mg_ref, w_ref, m_ref, acc_ref):
    k = pl.program_id(1)

    def branch(o_ref, first):
        t = (o_ref[...].astype(F32) * _silu(g_ref[...].astype(F32))).astype(BF16)
        r = _dot(t, w_ref[0]) * jax.nn.sigmoid(mg_ref[...].astype(F32))
        if first:
            acc_ref[...] = r
        else:
            acc_ref[...] += r

    for kk, o_ref in enumerate((o0_ref, o1_ref, o2_ref, o3_ref)):
        pl.when(k == kk)(functools.partial(branch, o_ref, kk == 0))

    @pl.when(k == N_BRANCH - 1)
    def _():
        m_ref[...] = acc_ref[...].astype(m_ref.dtype)


def _combine(p, outs, w_p, tm):
    t = p.shape[0]
    o_spec = pl.BlockSpec((tm, 1024), lambda i, k: (i, 0))
    return pl.pallas_call(
        _combine_kernel,
        out_shape=jax.ShapeDtypeStruct((t, D_MODEL), BF16),
        grid=(t // tm, N_BRANCH),
        in_specs=[o_spec, o_spec, o_spec, o_spec,
                  pl.BlockSpec((tm, 1024), lambda i, k: (i, C_G // 1024 + k)),
                  pl.BlockSpec((tm, D_MODEL), lambda i, k: (i, C_MG // D_MODEL + k)),
                  pl.BlockSpec((1, 1024, D_MODEL), lambda i, k: (k, 0, 0))],
        out_specs=pl.BlockSpec((tm, D_MODEL), lambda i, k: (i, 0)),
        scratch_shapes=[pltpu.VMEM((tm, D_MODEL), F32)],
        compiler_params=_cparams("arbitrary", "arbitrary"),
        name="combine",
    )(*outs, p, p, w_p)


def _outproj_kernel(m_ref, w_ref, x_ref, gt_ref, g_ref, o_ref):
    y = _dot(m_ref[...], w_ref[...])
    o_ref[...] = x_ref[...] + gt_ref[0] * _rms(y, g_ref[...])


def _outproj(m, w_out, x2, mod3, row_of_tile, g_post, tm):
    t = m.shape[0]
    return pl.pallas_call(
        _outproj_kernel,
        out_shape=jax.ShapeDtypeStruct((t, D_MODEL), F32),
        grid=(t // tm,),
        in_specs=[pl.BlockSpec((tm, D_MODEL), lambda i: (i, 0)),
                  pl.BlockSpec((D_MODEL, D_MODEL), lambda i: (0, 0), pipeline_mode=pl.Buffered(1)),
                  pl.BlockSpec((tm, D_MODEL), lambda i: (i, 0)),
                  pl.BlockSpec((1, 1, D_MODEL), lambda i: (row_of_tile(i), 0, 2)),
                  pl.BlockSpec((1, D_MODEL), lambda i: (0, 0))],
        out_specs=pl.BlockSpec((tm, D_MODEL), lambda i: (i, 0)),
        compiler_params=_cparams("arbitrary"),
        name="outproj",
    )(m, w_out, x2, mod3, g_post)


def _rope_swap(w):
    return jnp.concatenate([-w[..., 16:32], w[..., 0:16], -w[..., 48:64], w[..., 32:48]], axis=-1)


def _prep_w_tail(wt_in, layer):
    o_qc = 2368
    kr = wt_in[layer, KV_LORA:KV_LORA + MLA_ROPE]
    krs = jnp.concatenate([-kr[16:32], kr[0:16], -kr[48:64], kr[32:48]], axis=0)
    parts = [wt_in[layer, o_qc:o_qc + Q_LORA], wt_in[layer, :KV_LORA + MLA_ROPE], krs,
             jnp.zeros((128, D_MODEL), wt_in.dtype)]
    return jnp.concatenate(parts, axis=0).astype(BF16)


def _prep_w_uq(w):
    w3 = w.reshape(Q_LORA, MLA_HEADS, MLA_NOPE + MLA_ROPE)
    nope = w3[:, :, :MLA_NOPE].reshape(Q_LORA, MLA_W)
    r = w3[:, :, MLA_NOPE:]
    rr = jnp.concatenate([r, _rope_swap(r)], axis=-1).reshape(Q_LORA, MLA_HEADS * 128)
    return jnp.concatenate([nope, rr], axis=1).astype(BF16)


def _prep_w_ukv(w):
    w3 = w.reshape(KV_LORA, MLA_HEADS, MLA_NOPE + MLA_V)
    return jnp.concatenate([w3[:, :, :MLA_NOPE].reshape(KV_LORA, MLA_W),
                            w3[:, :, MLA_NOPE:].reshape(KV_LORA, MLA_W)], axis=1).astype(BF16)


def _rope_table(n):
    nf = MLA_ROPE // 4
    t = jnp.arange(n, dtype=jnp.int32)
    pos = jnp.stack([t // GRID_W, t % GRID_W], axis=-1).astype(F32)
    inv = ROPE_THETA ** (-jnp.arange(nf, dtype=F32) / nf)
    ang = pos[:, :, None] * inv
    cos = jnp.repeat(jnp.cos(ang), 2, axis=1).reshape(n, MLA_ROPE)
    sin = jnp.repeat(jnp.sin(ang), 2, axis=1).reshape(n, MLA_ROPE)
    return jnp.concatenate([cos, sin], axis=-1).astype(F32)


def _identity_rope_table(n):
    return jnp.concatenate([jnp.ones((n, MLA_ROPE), F32), jnp.zeros((n, MLA_ROPE), F32)], axis=-1)


def kernel(x, c, ctx, c_ctx, g_pre, g_post, w_ada, b_ada, w_in, g_q, g_kv, w_uq, w_ukv, conv_w, na_rpb,
           w_p_conv, w_p_mla, w_p_na, w_p_fnet, w_out):
    n_batch, n_seq, _ = x.shape
    n_ctx = ctx.shape[1]
    assert n_batch + 1 <= 8 and n_seq % 1024 == 0 and n_ctx % 256 == 0
    xl = x.reshape(n_batch * n_seq, D_MODEL)
    xc = ctx.reshape(n_batch * n_ctx, D_MODEL)

    cc = jnp.concatenate([c, c_ctx[None], jnp.zeros((8 - n_batch - 1, D_MODEL), c.dtype)], axis=0)
    mod = _ada(cc, w_ada, b_ada.reshape(DEPTH, 1, 3 * D_MODEL))

    wt_in = jnp.swapaxes(w_in, 1, 2)
    rope_l = _rope_table(n_seq)
    rope_c = _identity_rope_table(n_ctx)
    tm_l = 1024
    tm_c = n_batch * n_ctx
    ctx_row = lambda i: n_batch

    def lat_row_t(tm):
        return lambda i: i * tm // n_seq

    for l in range(DEPTH):
        ctx_out = l < DEPTH - 1
        mod3 = mod[l].reshape(8, 1, 3 * D_MODEL)
        w_tail = _prep_w_tail(wt_in, l)
        w_q = _prep_w_uq(w_uq[l])
        w_kv = _prep_w_ukv(w_ukv[l])
        gq = g_q[l].reshape(1, Q_LORA)
        gkv = g_kv[l].reshape(1, KV_LORA)
        gpre = g_pre[l].reshape(1, D_MODEL)
        gpost = g_post[l].reshape(1, D_MODEL)
        w_p = jnp.stack([w_p_conv[l], w_p_mla[l], w_p_na[l], w_p_fnet[l]]).astype(BF16)
        w_o = w_out[l].astype(BF16)
        bias = _na_bias_table(na_rpb[l])

        all_tiles = tuple(range(NP_FULL // TN))
        ul = _prenorm(xl, mod3, lat_row_t(512), gpre, 512)
        uc = _prenorm(xc, mod3, ctx_row, gpre, min(512, tm_c))
        pl_ = _inproj(ul, wt_in, l, w_tail, all_tiles, tm_l)
        if ctx_out:
            pc = _inproj(uc, wt_in, l, w_tail, all_tiles, tm_c)
            base = 0
        else:
            pc = _inproj(uc, wt_in, l, w_tail, all_tiles[KV_BASE // TN:], tm_c)
            base = KV_BASE

        kc_m, vc_m = _kvup(pc, C_KV - base, gkv, w_kv, rope_c, n_ctx, n_ctx)
        kl_m, vl_m = _kvup(pl_, C_KV, gkv, w_kv, rope_l, n_seq, 512)
        ql_m = _qup(pl_, C_QC, gq, w_q, rope_l, n_seq, 512)
        mla_l = _mla_attn(ql_m, kc_m, vc_m, kl_m, vl_m, n_batch, n_seq, n_ctx, 1024)
        na_l = _na_lat(pl_, pc, C_NQ, C_NK, C_NV, C_NK - base, C_NV - base, bias, n_batch, n_seq, n_ctx)
        conv_l = _conv(pl_, conv_w[l], n_batch, n_seq)
        fn_l = _fourier(pl_, n_batch, n_seq)
        m_l = _combine(pl_, (conv_l, mla_l, na_l, fn_l), w_p, 512)
        xl_new = _outproj(m_l, w_o, xl, mod3, lat_row_t(512), gpost, 512)

        if ctx_out:
            qc_m = _qup(pc, C_QC, gq, w_q, rope_c, n_ctx, n_ctx)
            mla_c = _mla_attn(qc_m, kc_m, vc_m, None, None, n_batch, n_ctx, n_ctx, n_ctx)
            na_c = _na_ctx(pc, C_NQ, C_NK, C_NV, n_batch, n_ctx)
            conv_c = _conv(pc, conv_w[l], n_batch, n_ctx)
            fn_c = _fourier(pc, n_batch, n_ctx)
            m_c = _combine(pc, (conv_c, mla_c, na_c, fn_c), w_p, min(512, tm_c))
            xc = _outproj(m_c, w_o, xc, mod3, ctx_row, gpost, min(512, tm_c))
        xl = xl_new

    return xl.reshape(n_batch, n_seq, D_MODEL)
```

```python
import functools

import jax
import jax.numpy as jnp
import numpy as np
from jax import lax
from jax.experimental import pallas as pl
from jax.experimental.pallas import tpu as pltpu

F32 = jnp.float32
BF16 = jnp.bfloat16

D_MODEL = 2048
DEPTH = 2
GRID_W = 64
CONV_W = 1024
CONV_K = 3
MLA_HEADS = 8
MLA_NOPE = 128
MLA_ROPE = 64
MLA_V = 128
Q_LORA = 512
KV_LORA = 256
MLA_W = MLA_HEADS * MLA_V
MLA_SCALE = (MLA_NOPE + MLA_ROPE) ** -0.5
NA_HEADS = 16
NA_HD = 64
NA_W = NA_HEADS * NA_HD
NA_KH = 8
NA_KW = 16
NA_SCALE = NA_HD ** -0.5
FNET_GROUPS = 4
FNET_GW = 256
FNET_W = FNET_GROUPS * FNET_GW
N_BRANCH = 4
ROPE_THETA = 10000.0
EPS = 1e-6
NEG = -1e30

VMEM_LIMIT_BYTES = 56 * 1024 * 1024

MLA_QK = 256
KV_SEG = 512
C_MG, C_G, C_CB, C_CC, C_CX, C_FV, C_NQ, C_NK, C_NV, C_QC, C_KV = (
    0, 8192, 12288, 13312, 14336, 15360, 16384, 17408, 18432, 19456, 19968)
NP_FULL = 20480
KV_BASE = C_NK


def _cparams(*sem):
    return pltpu.CompilerParams(dimension_semantics=sem, vmem_limit_bytes=VMEM_LIMIT_BYTES)


def _rms(x, g):
    return x * lax.rsqrt(jnp.mean(x * x, axis=-1, keepdims=True) + EPS) * g


def _silu(x):
    return x * jax.nn.sigmoid(x)


def _dot(a, b):
    return jnp.dot(a, b, preferred_element_type=F32)


def _dot_nt(a, b):
    return lax.dot_general(a, b, (((1,), (1,)), ((), ())), preferred_element_type=F32)


def _ada_kernel(c_ref, w_ref, b_ref, o_ref):
    a = _silu(c_ref[...]).astype(BF16)
    o_ref[0] = _dot(a, w_ref[0].astype(BF16)) + b_ref[0]


def _ada(cc, w_ada, b_ada):
    tn = 1024
    return pl.pallas_call(
        _ada_kernel,
        out_shape=jax.ShapeDtypeStruct((DEPTH, 8, 3 * D_MODEL), F32),
        grid=(DEPTH, 3 * D_MODEL // tn),
        in_specs=[pl.BlockSpec((8, D_MODEL), lambda l, j: (0, 0)),
                  pl.BlockSpec((1, D_MODEL, tn), lambda l, j: (l, 0, j)),
                  pl.BlockSpec((1, 1, tn), lambda l, j: (l, 0, j))],
        out_specs=pl.BlockSpec((1, 8, tn), lambda l, j: (l, 0, j)),
        compiler_params=_cparams("arbitrary", "arbitrary"),
        name="ada",
    )(cc, w_ada, b_ada)


def _prenorm_kernel(x_ref, sh_ref, sc_ref, g_ref, u_ref):
    y = _rms(x_ref[...], g_ref[...])
    u_ref[...] = (y * (1.0 + sc_ref[0]) + sh_ref[0]).astype(BF16)


def _prenorm(x2, mod3, row_of_tile, g_pre, tm):
    t = x2.shape[0]
    return pl.pallas_call(
        _prenorm_kernel,
        out_shape=jax.ShapeDtypeStruct((t, D_MODEL), BF16),
        grid=(t // tm,),
        in_specs=[pl.BlockSpec((tm, D_MODEL), lambda i: (i, 0)),
                  pl.BlockSpec((1, 1, D_MODEL), lambda i: (row_of_tile(i), 0, 0)),
                  pl.BlockSpec((1, 1, D_MODEL), lambda i: (row_of_tile(i), 0, 1)),
                  pl.BlockSpec((1, D_MODEL), lambda i: (0, 0))],
        out_specs=pl.BlockSpec((tm, D_MODEL), lambda i: (i, 0)),
        compiler_params=_cparams("arbitrary"),
        name="prenorm",
    )(x2, mod3, mod3, g_pre)


TN = 1024
W_ROWS = 256
N_ORIG = 20288

_TILE_SRC = ([12096 + TN * t for t in range(8)] + [8000 + TN * t for t in range(4)]
             + [3904 + TN * t for t in range(3)] + [6976, 2880, 320, 1344, -1])
SRC_QC = 2368
KV_ROWS = KV_LORA + MLA_ROPE


def _inproj_kernel(src_ref, u_ref, wt_hbm, o_ref, stage, wbuf, sems, *, layer, n_tiles):
    j = pl.program_id(0)

    def main_copy(jj):
        start = pl.multiple_of(src_ref[jj], 8)
        return pltpu.make_async_copy(wt_hbm.at[layer, pl.ds(start, TN), :], stage, sems.at[0])

    def tail_copies():
        return (pltpu.make_async_copy(wt_hbm.at[layer, pl.ds(SRC_QC, Q_LORA), :],
                                      stage.at[pl.ds(0, Q_LORA), :], sems.at[0]),
                pltpu.make_async_copy(wt_hbm.at[layer, pl.ds(0, KV_ROWS), :],
                                      stage.at[pl.ds(Q_LORA, KV_ROWS), :], sems.at[1]))

    def start(jj):
        @pl.when(src_ref[jj] >= 0)
        def _():
            main_copy(jj).start()

        @pl.when(src_ref[jj] < 0)
        def _():
            for c in tail_copies():
                c.start()

    def convert(n_rows):
        def body(r, carry):
            r0 = pl.multiple_of(r * W_ROWS, W_ROWS)
            wbuf[pl.ds(r0, W_ROWS), :] = stage[pl.ds(r0, W_ROWS), :].astype(BF16)
            return carry

        lax.fori_loop(0, n_rows // W_ROWS, body, 0)

    @pl.when(pl.program_id(1) == 0)
    def _():
        @pl.when(j == 0)
        def _():
            start(0)

        @pl.when(src_ref[j] >= 0)
        def _():
            main_copy(j).wait()
            convert(TN)

        @pl.when(src_ref[j] < 0)
        def _():
            for c in tail_copies():
                c.wait()
            convert(768)
            kr0 = Q_LORA + KV_LORA
            wbuf[768:832, :] = stage[768:832, :].astype(BF16)
            wbuf[832:848, :] = (-stage[kr0 + 16:kr0 + 32, :]).astype(BF16)
            wbuf[848:864, :] = stage[kr0:kr0 + 16, :].astype(BF16)
            wbuf[864:880, :] = (-stage[kr0 + 48:kr0 + 64, :]).astype(BF16)
            wbuf[880:896, :] = stage[kr0 + 32:kr0 + 48, :].astype(BF16)
            wbuf[896:TN, :] = jnp.zeros((TN - 896, D_MODEL), BF16)

        @pl.when(j + 1 < n_tiles)
        def _():
            start(j + 1)

    o_ref[...] = _dot_nt(u_ref[...], wbuf[...]).astype(o_ref.dtype)


def _inproj(u, wt_in, layer, tiles, tm):
    t = u.shape[0]
    src = np.array([_TILE_SRC[j] for j in tiles], np.int32)
    assert all(s % 8 == 0 and s + TN <= N_ORIG for s in src[src >= 0])
    grid_spec = pltpu.PrefetchScalarGridSpec(
        num_scalar_prefetch=1,
        grid=(len(tiles), t // tm),
        in_specs=[pl.BlockSpec((tm, D_MODEL), lambda j, i, s: (i, 0)),
                  pl.BlockSpec(memory_space=pl.ANY)],
        out_specs=pl.BlockSpec((tm, TN), lambda j, i, s: (i, j)),
        scratch_shapes=[pltpu.VMEM((TN, D_MODEL), F32), pltpu.VMEM((TN, D_MODEL), BF16),
                        pltpu.SemaphoreType.DMA((2,))])
    return pl.pallas_call(
        functools.partial(_inproj_kernel, layer=layer, n_tiles=len(tiles)),
        out_shape=jax.ShapeDtypeStruct((t, len(tiles) * TN), BF16),
        grid_spec=grid_spec,
        compiler_params=_cparams("arbitrary", "arbitrary"),
        name="inproj",
    )(jnp.asarray(src), u, wt_in)


def _qup_kernel(h_ref, g_ref, w_ref, cs_ref, o_ref):
    hn = _rms(h_ref[...].astype(F32), g_ref[...]).astype(BF16)
    qq = _dot(hn, w_ref[...]) * MLA_SCALE
    cs = cs_ref[...]
    for h in range(MLA_HEADS):
        o_ref[:, h * MLA_QK:h * MLA_QK + MLA_NOPE] = qq[:, h * 128:(h + 1) * 128].astype(BF16)
        t = qq[:, MLA_W + h * 128:MLA_W + (h + 1) * 128] * cs
        o_ref[:, h * MLA_QK + MLA_NOPE:(h + 1) * MLA_QK] = (t + pltpu.roll(t, 64, 1)).astype(BF16)


def _qup(p, col, g_q, w_q, cs, n_seq, tm):
    t = p.shape[0]
    spt = n_seq // tm
    return pl.pallas_call(
        _qup_kernel,
        out_shape=jax.ShapeDtypeStruct((t, MLA_HEADS * MLA_QK), BF16),
        grid=(t // tm,),
        in_specs=[pl.BlockSpec((tm, Q_LORA), lambda i: (i, col // Q_LORA)),
                  pl.BlockSpec((1, Q_LORA), lambda i: (0, 0)),
                  pl.BlockSpec((Q_LORA, 2 * MLA_W), lambda i: (0, 0)),
                  pl.BlockSpec((tm, 128), lambda i: (i % spt, 0))],
        out_specs=pl.BlockSpec((tm, MLA_HEADS * MLA_QK), lambda i: (i, 0)),
        compiler_params=_cparams("arbitrary"),
        name="mla_qup",
    )(p, g_q, w_q, cs)


def _kvup_kernel(p_ref, g_ref, w_ref, cs_ref, k_ref, v_ref):
    p = p_ref[...]
    cn = _rms(p[:, :KV_LORA].astype(F32), g_ref[...]).astype(BF16)
    kv = _dot(cn, w_ref[...])
    t = p[:, KV_LORA:KV_LORA + 128].astype(F32) * cs_ref[...]
    lane = lax.broadcasted_iota(jnp.int32, t.shape, 1)
    kr = jnp.where(lane < MLA_ROPE, t + pltpu.roll(t, 64, 1), 0.0).astype(BF16)
    for h in range(MLA_HEADS):
        k_ref[:, h * MLA_QK:h * MLA_QK + MLA_NOPE] = kv[:, h * 128:(h + 1) * 128].astype(BF16)
        k_ref[:, h * MLA_QK + MLA_NOPE:(h + 1) * MLA_QK] = kr
    v_ref[...] = kv[:, MLA_W:].astype(BF16)


def _kvup(p, col, g_kv, w_kv, cs, n_seq, tm):
    t = p.shape[0]
    spt = n_seq // tm
    return pl.pallas_call(
        _kvup_kernel,
        out_shape=(jax.ShapeDtypeStruct((t, MLA_HEADS * MLA_QK), BF16),
                   jax.ShapeDtypeStruct((t, MLA_W), BF16)),
        grid=(t // tm,),
        in_specs=[pl.BlockSpec((tm, KV_SEG), lambda i: (i, col // KV_SEG)),
                  pl.BlockSpec((1, KV_LORA), lambda i: (0, 0)),
                  pl.BlockSpec((KV_LORA, 2 * MLA_W), lambda i: (0, 0)),
                  pl.BlockSpec((tm, 128), lambda i: (i % spt, 0))],
        out_specs=(pl.BlockSpec((tm, MLA_HEADS * MLA_QK), lambda i: (i, 0)),
                   pl.BlockSpec((tm, MLA_W), lambda i: (i, 0))),
        compiler_params=_cparams("arbitrary"),
        name="mla_kvup",
    )(p, g_kv, w_kv, cs)


MLA_ROWS = 256


def _mla_attn_kernel(q_ref, kc_ref, vc_ref, *rest, has_lat):
    if has_lat:
        kl_ref, vl_ref, o_ref = rest
    else:
        (o_ref,) = rest
    for r0 in range(0, q_ref.shape[0], MLA_ROWS):
        rows = slice(r0, r0 + MLA_ROWS)
        q = q_ref[rows, :]
        s_c = _dot_nt(q, kc_ref[...])
        m = jnp.max(s_c, axis=-1, keepdims=True)
        if has_lat:
            s_l = _dot_nt(q, kl_ref[...])
            m = jnp.maximum(m, jnp.max(s_l, axis=-1, keepdims=True))
            p_l = jnp.exp(s_l - m)
        p_c = jnp.exp(s_c - m)
        den = jnp.sum(p_c, axis=-1, keepdims=True)
        o = _dot(p_c.astype(BF16), vc_ref[...])
        if has_lat:
            den = den + jnp.sum(p_l, axis=-1, keepdims=True)
            o = o + _dot(p_l.astype(BF16), vl_ref[...])
        o_ref[rows, :] = (o / den).astype(o_ref.dtype)


def _mla_attn(q, kc, vc, kl, vl, n_batch, n_q, n_ctx, tq):
    has_lat = kl is not None
    qt = n_q // tq
    in_specs = [pl.BlockSpec((tq, MLA_QK), lambda b, h, i: (b * qt + i, h)),
                pl.BlockSpec((n_ctx, MLA_QK), lambda b, h, i: (b, h)),
                pl.BlockSpec((n_ctx, MLA_V), lambda b, h, i: (b, h))]
    args = [q, kc, vc]
    if has_lat:
        in_specs += [pl.BlockSpec((n_q, MLA_QK), lambda b, h, i: (b, h)),
                     pl.BlockSpec((n_q, MLA_V), lambda b, h, i: (b, h))]
        args += [kl, vl]
    return pl.pallas_call(
        functools.partial(_mla_attn_kernel, has_lat=has_lat),
        out_shape=jax.ShapeDtypeStruct((n_batch * n_q, MLA_W), BF16),
        grid=(n_batch, MLA_HEADS, qt),
        in_specs=in_specs,
        out_specs=pl.BlockSpec((tq, MLA_V), lambda b, h, i: (b * qt + i, h)),
        compiler_params=_cparams("arbitrary", "arbitrary", "arbitrary"),
        name="mla_attn_lat" if has_lat else "mla_attn_ctx",
    )(*args)


def _pair_rows(q2):
    lane = lax.broadcasted_iota(jnp.int32, q2.shape, 1)
    zero = jnp.zeros_like(q2)
    return jnp.concatenate([jnp.where(lane < NA_HD, q2, zero), jnp.where(lane >= NA_HD, q2, zero)], axis=0)


def _unpair_rows(o, n):
    lane = lax.broadcasted_iota(jnp.int32, (n, 128), 1)
    return jnp.where(lane < NA_HD, o[:n], o[n:])


def _na_lat_kernel(q_ref, k_ref, v_ref, kc_ref, vc_ref, b_ref, o_ref):
    r = pl.program_id(1)
    rows = k_ref.shape[0] // GRID_W
    rs = jnp.clip(r - NA_KH // 2, 0, rows - NA_KH)
    rel = r - rs
    start = pl.multiple_of(rs * GRID_W, GRID_W)
    for hp in range(NA_HEADS // 2):
        cols = slice(hp * 128, (hp + 1) * 128)
        qm = _pair_rows(q_ref[:, cols] * NA_SCALE)
        k2 = k_ref[pl.ds(start, NA_KH * GRID_W), cols]
        v2 = v_ref[pl.ds(start, NA_KH * GRID_W), cols]
        bias = jnp.concatenate([b_ref[hp, NA_KH - 1 - rel + i] for i in range(0, NA_KH, 2)], axis=1)
        s_nb = _dot_nt(qm, k2) + bias
        s_cx = _dot_nt(qm, kc_ref[:, cols])
        m = jnp.maximum(jnp.max(s_nb, axis=-1, keepdims=True), jnp.max(s_cx, axis=-1, keepdims=True))
        p_nb = jnp.exp(s_nb - m)
        p_cx = jnp.exp(s_cx - m)
        den = jnp.sum(p_nb, axis=-1, keepdims=True) + jnp.sum(p_cx, axis=-1, keepdims=True)
        o = (_dot(p_nb.astype(BF16), v2) + _dot(p_cx.astype(BF16), vc_ref[:, cols])) / den
        o_ref[:, cols] = _unpair_rows(o, GRID_W).astype(o_ref.dtype)


def _na_lat(p, pc, cq, ck, cv, cck, ccv, bias, n_batch, n_seq, n_ctx):
    rows = n_seq // GRID_W
    return pl.pallas_call(
        _na_lat_kernel,
        out_shape=jax.ShapeDtypeStruct((n_batch * n_seq, NA_W), BF16),
        grid=(n_batch, rows),
        in_specs=[pl.BlockSpec((GRID_W, NA_W), lambda b, r: (b * rows + r, cq // NA_W)),
                  pl.BlockSpec((n_seq, NA_W), lambda b, r: (b, ck // NA_W)),
                  pl.BlockSpec((n_seq, NA_W), lambda b, r: (b, cv // NA_W)),
                  pl.BlockSpec((n_ctx, NA_W), lambda b, r: (b, cck // NA_W)),
                  pl.BlockSpec((n_ctx, NA_W), lambda b, r: (b, ccv // NA_W)),
                  pl.BlockSpec((NA_HEADS // 2, 2 * NA_KH - 2, 2 * GRID_W, 2 * GRID_W), lambda b, r: (0, 0, 0, 0),
                               pipeline_mode=pl.Buffered(1))],
        out_specs=pl.BlockSpec((GRID_W, NA_W), lambda b, r: (b * rows + r, 0)),
        compiler_params=_cparams("arbitrary", "arbitrary"),
        name="na_lat",
    )(p, p, p, pc, pc, bias)


def _na_ctx_kernel(q_ref, kc_ref, vc_ref, o_ref):
    n = q_ref.shape[0]
    for hp in range(NA_HEADS // 2):
        cols = slice(hp * 128, (hp + 1) * 128)
        qm = _pair_rows(q_ref[:, cols] * NA_SCALE)
        s = _dot_nt(qm, kc_ref[:, cols])
        p = jnp.exp(s - jnp.max(s, axis=-1, keepdims=True))
        o = _dot(p.astype(BF16), vc_ref[:, cols]) / jnp.sum(p, axis=-1, keepdims=True)
        o_ref[:, cols] = _unpair_rows(o, n).astype(o_ref.dtype)


def _na_ctx(pc, cq, ck, cv, n_batch, n_ctx):
    return pl.pallas_call(
        _na_ctx_kernel,
        out_shape=jax.ShapeDtypeStruct((n_batch * n_ctx, NA_W), BF16),
        grid=(n_batch,),
        in_specs=[pl.BlockSpec((n_ctx, NA_W), lambda b: (b, cq // NA_W)),
                  pl.BlockSpec((n_ctx, NA_W), lambda b: (b, ck // NA_W)),
                  pl.BlockSpec((n_ctx, NA_W), lambda b: (b, cv // NA_W))],
        out_specs=pl.BlockSpec((n_ctx, NA_W), lambda b: (b, 0)),
        compiler_params=_cparams("arbitrary"),
        name="na_ctx",
    )(pc, pc, pc)


def _na_bias_table(rpb):
    c = np.arange(GRID_W)[:, None]
    kc = np.arange(GRID_W)[None, :]
    cs = np.clip(c - NA_KW // 2, 0, GRID_W - NA_KW)
    valid = (kc >= cs) & (kc < cs + NA_KW)
    co = kc - c + (NA_KW - 1)
    onehot = (valid[None] & (co[None] == np.arange(2 * NA_KW - 1)[:, None, None])).astype(np.float32)
    t = jnp.einsum('hrj,jck->hrck', rpb.astype(F32), onehot, precision=lax.Precision.HIGHEST)
    t = t + np.where(valid, 0.0, NEG).astype(np.float32)
    t = t.reshape(NA_HEADS // 2, 2, 2 * NA_KH - 1, GRID_W, GRID_W).transpose(0, 2, 1, 3, 4)
    t = t.reshape(NA_HEADS // 2, 2 * NA_KH - 1, 2 * GRID_W, GRID_W)
    return jnp.concatenate([t[:, :-1], t[:, 1:]], axis=-1)


def _conv_kernel(cb_ref, cc_ref, cx_ref, w_ref, o_ref):
    z = cc_ref[...].astype(F32) * cx_ref[...].astype(F32)
    n = z.shape[0]
    row = lax.broadcasted_iota(jnp.int32, z.shape, 0)
    zp = jnp.where(row == 0, 0.0, pltpu.roll(z, 1, 0))
    zn = jnp.where(row == n - 1, 0.0, pltpu.roll(z, n - 1, 0))
    w = w_ref[...]
    conv = zp * w[0:1] + z * w[1:2] + zn * w[2:3]
    o_ref[...] = (cb_ref[...].astype(F32) * conv).astype(o_ref.dtype)


def _conv(p, conv_w, n_batch, n_seq):
    tc = 256
    nb = CONV_W // tc
    return pl.pallas_call(
        _conv_kernel,
        out_shape=jax.ShapeDtypeStruct((n_batch * n_seq, CONV_W), BF16),
        grid=(n_batch, nb),
        in_specs=[pl.BlockSpec((n_seq, tc), lambda b, j: (b, C_CB // tc + j)),
                  pl.BlockSpec((n_seq, tc), lambda b, j: (b, C_CC // tc + j)),
                  pl.BlockSpec((n_seq, tc), lambda b, j: (b, C_CX // tc + j)),
                  pl.BlockSpec((CONV_K, tc), lambda b, j: (0, j))],
        out_specs=pl.BlockSpec((n_seq, tc), lambda b, j: (b, j)),
        compiler_params=_cparams("arbitrary", "arbitrary"),
        name="conv",
    )(p, p, p, conv_w)


def _fourier_kernel(x_ref, cg_ref, sg_ref, cs_ref, ss_ref, o_ref, *, norm):
    x = x_ref[...]
    a = _dot(x, cg_ref[...]).astype(BF16)
    b = _dot(x, sg_ref[...]).astype(BF16)
    o = _dot(cs_ref[...], a) - _dot(ss_ref[...], b)
    o_ref[...] = (o * norm).astype(o_ref.dtype)


@functools.lru_cache(maxsize=None)
def _dft_tables(n):
    j = np.arange(n, dtype=np.int64)
    ang = ((j[:, None] * j[None, :]) % n).astype(np.float64) * (2.0 * np.pi / n)
    return np.cos(ang).astype(np.float32), np.sin(ang).astype(np.float32)


def _fourier(p, n_batch, n_seq):
    cg, sg = (jnp.asarray(t).astype(BF16) for t in _dft_tables(FNET_GW))
    cs, ss = (jnp.asarray(t).astype(BF16) for t in _dft_tables(n_seq))
    const = lambda b, g: (0, 0)
    return pl.pallas_call(
        functools.partial(_fourier_kernel, norm=float((n_seq * FNET_GW) ** -0.5)),
        out_shape=jax.ShapeDtypeStruct((n_batch * n_seq, FNET_W), BF16),
        grid=(n_batch, FNET_GROUPS),
        in_specs=[pl.BlockSpec((n_seq, FNET_GW), lambda b, g: (b, C_FV // FNET_GW + g)),
                  pl.BlockSpec((FNET_GW, FNET_GW), const),
                  pl.BlockSpec((FNET_GW, FNET_GW), const),
                  pl.BlockSpec((n_seq, n_seq), const, pipeline_mode=pl.Buffered(1)),
                  pl.BlockSpec((n_seq, n_seq), const, pipeline_mode=pl.Buffered(1))],
        out_specs=pl.BlockSpec((n_seq, FNET_GW), lambda b, g: (b, g)),
        compiler_params=_cparams("arbitrary", "arbitrary"),
        name="fourier",
    )(p, cg, sg, cs, ss)


def _combine_kernel(o0_ref, o1_ref, o2_ref, o3_ref, g_ref, mg_ref, w_ref, m_ref):
    acc = None
    for k, o_ref in enumerate((o0_ref, o1_ref, o2_ref, o3_ref)):
        g = g_ref[:, k * 1024:(k + 1) * 1024].astype(F32)
        t = (o_ref[...].astype(F32) * _silu(g)).astype(BF16)
        gate = jax.nn.sigmoid(mg_ref[:, k * D_MODEL:(k + 1) * D_MODEL].astype(F32))
        r = _dot(t, w_ref[k]) * gate
        acc = r if acc is None else acc + r
    m_ref[...] = acc.astype(m_ref.dtype)


def _combine(p, outs, w_p, tm):
    t = p.shape[0]
    o_spec = pl.BlockSpec((tm, 1024), lambda i: (i, 0))
    return pl.pallas_call(
        _combine_kernel,
        out_shape=jax.ShapeDtypeStruct((t, D_MODEL), BF16),
        grid=(t // tm,),
        in_specs=[o_spec, o_spec, o_spec, o_spec,
                  pl.BlockSpec((tm, N_BRANCH * 1024), lambda i: (i, C_G // (N_BRANCH * 1024))),
                  pl.BlockSpec((tm, N_BRANCH * D_MODEL), lambda i: (i, C_MG // (N_BRANCH * D_MODEL))),
                  pl.BlockSpec((N_BRANCH, 1024, D_MODEL), lambda i: (0, 0, 0), pipeline_mode=pl.Buffered(1))],
        out_specs=pl.BlockSpec((tm, D_MODEL), lambda i: (i, 0)),
        compiler_params=_cparams("arbitrary"),
        name="combine",
    )(*outs, p, p, w_p)


def _outproj_kernel(m_ref, w_ref, x_ref, gt_ref, g_ref, o_ref):
    y = _dot(m_ref[...], w_ref[...])
    o_ref[...] = x_ref[...] + gt_ref[0] * _rms(y, g_ref[...])


def _outproj(m, w_out, x2, mod3, row_of_tile, g_post, tm):
    t = m.shape[0]
    return pl.pallas_call(
        _outproj_kernel,
        out_shape=jax.ShapeDtypeStruct((t, D_MODEL), F32),
        grid=(t // tm,),
        in_specs=[pl.BlockSpec((tm, D_MODEL), lambda i: (i, 0)),
                  pl.BlockSpec((D_MODEL, D_MODEL), lambda i: (0, 0), pipeline_mode=pl.Buffered(1)),
                  pl.BlockSpec((tm, D_MODEL), lambda i: (i, 0)),
                  pl.BlockSpec((1, 1, D_MODEL), lambda i: (row_of_tile(i), 0, 2)),
                  pl.BlockSpec((1, D_MODEL), lambda i: (0, 0))],
        out_specs=pl.BlockSpec((tm, D_MODEL), lambda i: (i, 0)),
        compiler_params=_cparams("arbitrary"),
        name="outproj",
    )(m, w_out, x2, mod3, g_post)


def _rope_swap(w):
    return jnp.concatenate([-w[..., 16:32], w[..., 0:16], -w[..., 48:64], w[..., 32:48]], axis=-1)


def _prep_w_uq(w):
    w3 = w.reshape(Q_LORA, MLA_HEADS, MLA_NOPE + MLA_ROPE)
    nope = w3[:, :, :MLA_NOPE].reshape(Q_LORA, MLA_W)
    r = w3[:, :, MLA_NOPE:]
    rr = jnp.concatenate([r, _rope_swap(r)], axis=-1).reshape(Q_LORA, MLA_HEADS * 128)
    return jnp.concatenate([nope, rr], axis=1).astype(BF16)


def _prep_w_ukv(w):
    w3 = w.reshape(KV_LORA, MLA_HEADS, MLA_NOPE + MLA_V)
    return jnp.concatenate([w3[:, :, :MLA_NOPE].reshape(KV_LORA, MLA_W),
                            w3[:, :, MLA_NOPE:].reshape(KV_LORA, MLA_W)], axis=1).astype(BF16)


def _rope_table(n):
    nf = MLA_ROPE // 4
    t = jnp.arange(n, dtype=jnp.int32)
    pos = jnp.stack([t // GRID_W, t % GRID_W], axis=-1).astype(F32)
    inv = ROPE_THETA ** (-jnp.arange(nf, dtype=F32) / nf)
    ang = pos[:, :, None] * inv
    cos = jnp.repeat(jnp.cos(ang), 2, axis=1).reshape(n, MLA_ROPE)
    sin = jnp.repeat(jnp.sin(ang), 2, axis=1).reshape(n, MLA_ROPE)
    return jnp.concatenate([cos, sin], axis=-1).astype(F32)


def _identity_rope_table(n):
    return jnp.concatenate([jnp.ones((n, MLA_ROPE), F32), jnp.zeros((n, MLA_ROPE), F32)], axis=-1)


def kernel(x, c, ctx, c_ctx, g_pre, g_post, w_ada, b_ada, w_in, g_q, g_kv, w_uq, w_ukv, conv_w, na_rpb,
           w_p_conv, w_p_mla, w_p_na, w_p_fnet, w_out):
    n_batch, n_seq, _ = x.shape
    n_ctx = ctx.shape[1]
    assert n_batch + 1 <= 8 and n_seq % 1024 == 0 and n_ctx % 256 == 0
    xl = x.reshape(n_batch * n_seq, D_MODEL)
    xc = ctx.reshape(n_batch * n_ctx, D_MODEL)

    cc = jnp.concatenate([c, c_ctx[None], jnp.zeros((8 - n_batch - 1, D_MODEL), c.dtype)], axis=0)
    mod = _ada(cc, w_ada, b_ada.reshape(DEPTH, 1, 3 * D_MODEL))

    wt_in = jnp.swapaxes(w_in, 1, 2)
    rope_l = _rope_table(n_seq)
    rope_c = _identity_rope_table(n_ctx)
    tm_l = 1024
    tm_c = n_batch * n_ctx
    ctx_row = lambda i: n_batch

    def lat_row_t(tm):
        return lambda i: i * tm // n_seq

    for l in range(DEPTH):
        ctx_out = l < DEPTH - 1
        mod3 = mod[l].reshape(8, 1, 3 * D_MODEL)
        w_q = _prep_w_uq(w_uq[l])
        w_kv = _prep_w_ukv(w_ukv[l])
        gq = g_q[l].reshape(1, Q_LORA)
        gkv = g_kv[l].reshape(1, KV_LORA)
        gpre = g_pre[l].reshape(1, D_MODEL)
        gpost = g_post[l].reshape(1, D_MODEL)
        w_p = jnp.stack([w_p_conv[l], w_p_mla[l], w_p_na[l], w_p_fnet[l]]).astype(BF16)
        w_o = w_out[l].astype(BF16)
        bias = _na_bias_table(na_rpb[l])

        all_tiles = tuple(range(NP_FULL // TN))
        ul = _prenorm(xl, mod3, lat_row_t(512), gpre, 512)
        uc = _prenorm(xc, mod3, ctx_row, gpre, min(512, tm_c))
        pl_ = _inproj(ul, wt_in, l, all_tiles, tm_l)
        if ctx_out:
            pc = _inproj(uc, wt_in, l, all_tiles, tm_c)
            base = 0
        else:
            pc = _inproj(uc, wt_in, l, all_tiles[KV_BASE // TN:], tm_c)
            base = KV_BASE

        kc_m, vc_m = _kvup(pc, C_KV - base, gkv, w_kv, rope_c, n_ctx, n_ctx)
        kl_m, vl_m = _kvup(pl_, C_KV, gkv, w_kv, rope_l, n_seq, 512)
        ql_m = _qup(pl_, C_QC, gq, w_q, rope_l, n_seq, 512)
        mla_l = _mla_attn(ql_m, kc_m, vc_m, kl_m, vl_m, n_batch, n_seq, n_ctx, 1024)
        na_l = _na_lat(pl_, pc, C_NQ, C_NK, C_NV, C_NK - base, C_NV - base, bias, n_batch, n_seq, n_ctx)
        conv_l = _conv(pl_, conv_w[l], n_batch, n_seq)
        fn_l = _fourier(pl_, n_batch, n_seq)
        m_l = _combine(pl_, (conv_l, mla_l, na_l, fn_l), w_p, 256)
        xl_new = _outproj(m_l, w_o, xl, mod3, lat_row_t(512), gpost, 512)

        if ctx_out:
            qc_m = _qup(pc, C_QC, gq, w_q, rope_c, n_ctx, n_ctx)
            mla_c = _mla_attn(qc_m, kc_m, vc_m, None, None, n_batch, n_ctx, n_ctx, n_ctx)
            na_c = _na_ctx(pc, C_NQ, C_NK, C_NV, n_batch, n_ctx)
            conv_c = _conv(pc, conv_w[l], n_batch, n_ctx)
            fn_c = _fourier(pc, n_batch, n_ctx)
            m_c = _combine(pc, (conv_c, mla_c, na_c, fn_c), w_p, 256)
            xc = _outproj(m_c, w_o, xc, mod3, ctx_row, gpost, min(512, tm_c))
        xl = xl_new

    return xl.reshape(n_batch, n_seq, D_MODEL)
```

```python
import functools

import jax
import jax.numpy as jnp
import numpy as np
from jax import lax
from jax.experimental import pallas as pl
from jax.experimental.pallas import tpu as pltpu

F32 = jnp.float32
BF16 = jnp.bfloat16

D_MODEL = 2048
DEPTH = 2
GRID_W = 64
CONV_W = 1024
CONV_K = 3
MLA_HEADS = 8
MLA_NOPE = 128
MLA_ROPE = 64
MLA_V = 128
Q_LORA = 512
KV_LORA = 256
MLA_W = MLA_HEADS * MLA_V
MLA_SCALE = (MLA_NOPE + MLA_ROPE) ** -0.5
NA_HEADS = 16
NA_HD = 64
NA_W = NA_HEADS * NA_HD
NA_KH = 8
NA_KW = 16
NA_SCALE = NA_HD ** -0.5
FNET_GROUPS = 4
FNET_GW = 256
FNET_W = FNET_GROUPS * FNET_GW
N_BRANCH = 4
ROPE_THETA = 10000.0
EPS = 1e-6
NEG = -1e30

VMEM_LIMIT_BYTES = 56 * 1024 * 1024

MLA_QK = 256
KV_SEG = 512
C_MG, C_G, C_CB, C_CC, C_CX, C_FV, C_NQ, C_NK, C_NV, C_QC, C_KV = (
    0, 8192, 12288, 13312, 14336, 15360, 16384, 17408, 18432, 19456, 19968)
NP_FULL = 20480
KV_BASE = C_NK


def _cparams(*sem):
    return pltpu.CompilerParams(dimension_semantics=sem, vmem_limit_bytes=VMEM_LIMIT_BYTES)


def _rms(x, g):
    return x * lax.rsqrt(jnp.mean(x * x, axis=-1, keepdims=True) + EPS) * g


def _silu(x):
    return x * jax.nn.sigmoid(x)


def _dot(a, b):
    return jnp.dot(a, b, preferred_element_type=F32)


def _dot_nt(a, b):
    return lax.dot_general(a, b, (((1,), (1,)), ((), ())), preferred_element_type=F32)


def _ada_kernel(c_ref, w_ref, b_ref, o_ref):
    a = _silu(c_ref[...]).astype(BF16)
    o_ref[0] = _dot(a, w_ref[0].astype(BF16)) + b_ref[0]


def _ada(cc, w_ada, b_ada):
    tn = 1024
    return pl.pallas_call(
        _ada_kernel,
        out_shape=jax.ShapeDtypeStruct((DEPTH, 8, 3 * D_MODEL), F32),
        grid=(DEPTH, 3 * D_MODEL // tn),
        in_specs=[pl.BlockSpec((8, D_MODEL), lambda l, j: (0, 0)),
                  pl.BlockSpec((1, D_MODEL, tn), lambda l, j: (l, 0, j)),
                  pl.BlockSpec((1, 1, tn), lambda l, j: (l, 0, j))],
        out_specs=pl.BlockSpec((1, 8, tn), lambda l, j: (l, 0, j)),
        compiler_params=_cparams("arbitrary", "arbitrary"),
        name="ada",
    )(cc, w_ada, b_ada)


def _prenorm_kernel(x_ref, sh_ref, sc_ref, g_ref, u_ref):
    y = _rms(x_ref[...], g_ref[...])
    u_ref[...] = (y * (1.0 + sc_ref[0]) + sh_ref[0]).astype(BF16)


def _prenorm(x2, mod3, row_of_tile, g_pre, tm):
    t = x2.shape[0]
    return pl.pallas_call(
        _prenorm_kernel,
        out_shape=jax.ShapeDtypeStruct((t, D_MODEL), BF16),
        grid=(t // tm,),
        in_specs=[pl.BlockSpec((tm, D_MODEL), lambda i: (i, 0)),
                  pl.BlockSpec((1, 1, D_MODEL), lambda i: (row_of_tile(i), 0, 0)),
                  pl.BlockSpec((1, 1, D_MODEL), lambda i: (row_of_tile(i), 0, 1)),
                  pl.BlockSpec((1, D_MODEL), lambda i: (0, 0))],
        out_specs=pl.BlockSpec((tm, D_MODEL), lambda i: (i, 0)),
        compiler_params=_cparams("arbitrary"),
        name="prenorm",
    )(x2, mod3, mod3, g_pre)


TN = 1024
W_ROWS = 256
N_ORIG = 20288

_TILE_SRC = ([12096 + TN * t for t in range(8)] + [8000 + TN * t for t in range(4)]
             + [3904 + TN * t for t in range(3)] + [6976, 2880, 320, 1344, -1])
SRC_QC = 2368
KV_ROWS = KV_LORA + MLA_ROPE


def _inproj_kernel(src_ref, u_ref, wt_hbm, o_ref, stage, wbuf, sems, *, layer, n_tiles):
    j = pl.program_id(0)

    def main_copy(jj):
        start = pl.multiple_of(src_ref[jj], 8)
        return pltpu.make_async_copy(wt_hbm.at[layer, pl.ds(start, TN), :], stage, sems.at[0])

    def tail_copies():
        return (pltpu.make_async_copy(wt_hbm.at[layer, pl.ds(SRC_QC, Q_LORA), :],
                                      stage.at[pl.ds(0, Q_LORA), :], sems.at[0]),
                pltpu.make_async_copy(wt_hbm.at[layer, pl.ds(0, KV_ROWS), :],
                                      stage.at[pl.ds(Q_LORA, KV_ROWS), :], sems.at[1]))

    def start(jj):
        @pl.when(src_ref[jj] >= 0)
        def _():
            main_copy(jj).start()

        @pl.when(src_ref[jj] < 0)
        def _():
            for c in tail_copies():
                c.start()

    def convert(n_rows):
        def body(r, carry):
            r0 = pl.multiple_of(r * W_ROWS, W_ROWS)
            wbuf[pl.ds(r0, W_ROWS), :] = stage[pl.ds(r0, W_ROWS), :].astype(BF16)
            return carry

        lax.fori_loop(0, n_rows // W_ROWS, body, 0)

    @pl.when(pl.program_id(1) == 0)
    def _():
        @pl.when(j == 0)
        def _():
            start(0)

        @pl.when(src_ref[j] >= 0)
        def _():
            main_copy(j).wait()
            convert(TN)

        @pl.when(src_ref[j] < 0)
        def _():
            for c in tail_copies():
                c.wait()
            convert(768)
            kr0 = Q_LORA + KV_LORA
            wbuf[768:832, :] = stage[768:832, :].astype(BF16)
            wbuf[832:848, :] = (-stage[kr0 + 16:kr0 + 32, :]).astype(BF16)
            wbuf[848:864, :] = stage[kr0:kr0 + 16, :].astype(BF16)
            wbuf[864:880, :] = (-stage[kr0 + 48:kr0 + 64, :]).astype(BF16)
            wbuf[880:896, :] = stage[kr0 + 32:kr0 + 48, :].astype(BF16)
            wbuf[896:TN, :] = jnp.zeros((TN - 896, D_MODEL), BF16)

        @pl.when(j + 1 < n_tiles)
        def _():
            start(j + 1)

    o_ref[...] = _dot_nt(u_ref[...], wbuf[...]).astype(o_ref.dtype)


def _inproj(u, wt_in, layer, tiles, tm):
    t = u.shape[0]
    src = np.array([_TILE_SRC[j] for j in tiles], np.int32)
    assert all(s % 8 == 0 and s + TN <= N_ORIG for s in src[src >= 0])
    grid_spec = pltpu.PrefetchScalarGridSpec(
        num_scalar_prefetch=1,
        grid=(len(tiles), t // tm),
        in_specs=[pl.BlockSpec((tm, D_MODEL), lambda j, i, s: (i, 0)),
                  pl.BlockSpec(memory_space=pl.ANY)],
        out_specs=pl.BlockSpec((tm, TN), lambda j, i, s: (i, j)),
        scratch_shapes=[pltpu.VMEM((TN, D_MODEL), F32), pltpu.VMEM((TN, D_MODEL), BF16),
                        pltpu.SemaphoreType.DMA((2,))])
    return pl.pallas_call(
        functools.partial(_inproj_kernel, layer=layer, n_tiles=len(tiles)),
        out_shape=jax.ShapeDtypeStruct((t, len(tiles) * TN), BF16),
        grid_spec=grid_spec,
        compiler_params=_cparams("arbitrary", "arbitrary"),
        name="inproj",
    )(jnp.asarray(src), u, wt_in)


def _qup_kernel(h_ref, g_ref, w_ref, cs_ref, o_ref):
    hn = _rms(h_ref[...].astype(F32), g_ref[...]).astype(BF16)
    qq = _dot(hn, w_ref[...]) * MLA_SCALE
    cs = cs_ref[...]
    for h in range(MLA_HEADS):
        o_ref[:, h * MLA_QK:h * MLA_QK + MLA_NOPE] = qq[:, h * 128:(h + 1) * 128].astype(BF16)
        t = qq[:, MLA_W + h * 128:MLA_W + (h + 1) * 128] * cs
        o_ref[:, h * MLA_QK + MLA_NOPE:(h + 1) * MLA_QK] = (t + pltpu.roll(t, 64, 1)).astype(BF16)


def _qup(p, col, g_q, w_q, cs, n_seq, tm):
    t = p.shape[0]
    spt = n_seq // tm
    return pl.pallas_call(
        _qup_kernel,
        out_shape=jax.ShapeDtypeStruct((t, MLA_HEADS * MLA_QK), BF16),
        grid=(t // tm,),
        in_specs=[pl.BlockSpec((tm, Q_LORA), lambda i: (i, col // Q_LORA)),
                  pl.BlockSpec((1, Q_LORA), lambda i: (0, 0)),
                  pl.BlockSpec((Q_LORA, 2 * MLA_W), lambda i: (0, 0)),
                  pl.BlockSpec((tm, 128), lambda i: (i % spt, 0))],
        out_specs=pl.BlockSpec((tm, MLA_HEADS * MLA_QK), lambda i: (i, 0)),
        compiler_params=_cparams("arbitrary"),
        name="mla_qup",
    )(p, g_q, w_q, cs)


def _kvup_kernel(p_ref, g_ref, w_ref, cs_ref, k_ref, v_ref):
    p = p_ref[...]
    cn = _rms(p[:, :KV_LORA].astype(F32), g_ref[...]).astype(BF16)
    kv = _dot(cn, w_ref[...])
    t = p[:, KV_LORA:KV_LORA + 128].astype(F32) * cs_ref[...]
    lane = lax.broadcasted_iota(jnp.int32, t.shape, 1)
    kr = jnp.where(lane < MLA_ROPE, t + pltpu.roll(t, 64, 1), 0.0).astype(BF16)
    for h in range(MLA_HEADS):
        k_ref[:, h * MLA_QK:h * MLA_QK + MLA_NOPE] = kv[:, h * 128:(h + 1) * 128].astype(BF16)
        k_ref[:, h * MLA_QK + MLA_NOPE:(h + 1) * MLA_QK] = kr
    v_ref[...] = kv[:, MLA_W:].astype(BF16)


def _kvup(p, col, g_kv, w_kv, cs, n_seq, tm):
    t = p.shape[0]
    spt = n_seq // tm
    return pl.pallas_call(
        _kvup_kernel,
        out_shape=(jax.ShapeDtypeStruct((t, MLA_HEADS * MLA_QK), BF16),
                   jax.ShapeDtypeStruct((t, MLA_W), BF16)),
        grid=(t // tm,),
        in_specs=[pl.BlockSpec((tm, KV_SEG), lambda i: (i, col // KV_SEG)),
                  pl.BlockSpec((1, KV_LORA), lambda i: (0, 0)),
                  pl.BlockSpec((KV_LORA, 2 * MLA_W), lambda i: (0, 0)),
                  pl.BlockSpec((tm, 128), lambda i: (i % spt, 0))],
        out_specs=(pl.BlockSpec((tm, MLA_HEADS * MLA_QK), lambda i: (i, 0)),
                   pl.BlockSpec((tm, MLA_W), lambda i: (i, 0))),
        compiler_params=_cparams("arbitrary"),
        name="mla_kvup",
    )(p, g_kv, w_kv, cs)


MLA_ROWS = 256


def _mla_attn_kernel(q_ref, kc_ref, vc_ref, *rest, has_lat):
    if has_lat:
        kl_ref, vl_ref, o_ref = rest
    else:
        (o_ref,) = rest
    for r0 in range(0, q_ref.shape[0], MLA_ROWS):
        rows = slice(r0, r0 + MLA_ROWS)
        q = q_ref[rows, :]
        s_c = _dot_nt(q, kc_ref[...])
        m = jnp.max(s_c, axis=-1, keepdims=True)
        if has_lat:
            s_l = _dot_nt(q, kl_ref[...])
            m = jnp.maximum(m, jnp.max(s_l, axis=-1, keepdims=True))
            p_l = jnp.exp(s_l - m)
        p_c = jnp.exp(s_c - m)
        den = jnp.sum(p_c, axis=-1, keepdims=True)
        o = _dot(p_c.astype(BF16), vc_ref[...])
        if has_lat:
            den = den + jnp.sum(p_l, axis=-1, keepdims=True)
            o = o + _dot(p_l.astype(BF16), vl_ref[...])
        o_ref[rows, :] = (o / den).astype(o_ref.dtype)


def _mla_attn(q, kc, vc, kl, vl, n_batch, n_q, n_ctx, tq):
    has_lat = kl is not None
    qt = n_q // tq
    in_specs = [pl.BlockSpec((tq, MLA_QK), lambda b, h, i: (b * qt + i, h)),
                pl.BlockSpec((n_ctx, MLA_QK), lambda b, h, i: (b, h)),
                pl.BlockSpec((n_ctx, MLA_V), lambda b, h, i: (b, h))]
    args = [q, kc, vc]
    if has_lat:
        in_specs += [pl.BlockSpec((n_q, MLA_QK), lambda b, h, i: (b, h)),
                     pl.BlockSpec((n_q, MLA_V), lambda b, h, i: (b, h))]
        args += [kl, vl]
    return pl.pallas_call(
        functools.partial(_mla_attn_kernel, has_lat=has_lat),
        out_shape=jax.ShapeDtypeStruct((n_batch * n_q, MLA_W), BF16),
        grid=(n_batch, MLA_HEADS, qt),
        in_specs=in_specs,
        out_specs=pl.BlockSpec((tq, MLA_V), lambda b, h, i: (b * qt + i, h)),
        compiler_params=_cparams("arbitrary", "arbitrary", "arbitrary"),
        name="mla_attn_lat" if has_lat else "mla_attn_ctx",
    )(*args)


def _pair_rows(q2):
    lane = lax.broadcasted_iota(jnp.int32, q2.shape, 1)
    zero = jnp.zeros_like(q2)
    return jnp.concatenate([jnp.where(lane < NA_HD, q2, zero), jnp.where(lane >= NA_HD, q2, zero)], axis=0)


def _unpair_rows(o, n):
    lane = lax.broadcasted_iota(jnp.int32, (n, 128), 1)
    return jnp.where(lane < NA_HD, o[:n], o[n:])


NA_G = 4
NA_SLAB = NA_G + NA_KH
NA_NEG_ENTRY = 2 * NA_KH


def _na_lat_kernel(q_ref, k_ref, v_ref, kc_ref, vc_ref, b_ref, o_ref):
    g = pl.program_id(1)
    rows = k_ref.shape[0] // GRID_W
    nq = NA_G * GRID_W
    us = jnp.clip(NA_G * g - NA_KH // 2, 0, rows - NA_SLAB)
    start = pl.multiple_of(us * GRID_W, GRID_W)

    entry, ok_l, ok_r = {}, {}, {}
    for dr in range(NA_G):
        r = NA_G * g + dr
        rs = jnp.clip(r - NA_KH // 2, 0, rows - NA_KH)
        for p in range(NA_SLAB // 2):
            kr = us + 2 * p
            l_ok = jnp.logical_and(kr >= rs, kr < rs + NA_KH)
            r_ok = jnp.logical_and(kr + 1 >= rs, kr + 1 < rs + NA_KH)
            ro = kr - r + (NA_KH - 1)
            entry[dr, p] = jnp.where(jnp.logical_or(l_ok, r_ok), jnp.clip(ro + 1, 0, NA_NEG_ENTRY - 1),
                                     NA_NEG_ENTRY)
            ok_l[dr, p] = l_ok.astype(jnp.int32)
            ok_r[dr, p] = r_ok.astype(jnp.int32)
    lane = lax.broadcasted_iota(jnp.int32, (GRID_W, 128), 1)

    def bias_tile(hp, sub, dr, p):
        t = b_ref[hp, entry[dr, p], sub * GRID_W:(sub + 1) * GRID_W, :]
        if dr % 2 == 1 and p in ((dr - 1) // 2, (dr + NA_KH - 1) // 2):
            t = jnp.where(jnp.where(lane < GRID_W, ok_l[dr, p], ok_r[dr, p]) > 0, t, NEG)
        return t

    for hp in range(NA_HEADS // 2):
        cols = slice(hp * 128, (hp + 1) * 128)
        qm = _pair_rows(q_ref[:, cols] * NA_SCALE)
        k2 = k_ref[pl.ds(start, NA_SLAB * GRID_W), cols]
        v2 = v_ref[pl.ds(start, NA_SLAB * GRID_W), cols]
        bias = jnp.concatenate(
            [jnp.concatenate([bias_tile(hp, sub, dr, p) for p in range(NA_SLAB // 2)], axis=1)
             for sub in range(2) for dr in range(NA_G)], axis=0)
        s_nb = _dot_nt(qm, k2) + bias
        s_cx = _dot_nt(qm, kc_ref[:, cols])
        m = jnp.maximum(jnp.max(s_nb, axis=-1, keepdims=True), jnp.max(s_cx, axis=-1, keepdims=True))
        p_nb = jnp.exp(s_nb - m)
        p_cx = jnp.exp(s_cx - m)
        den = jnp.sum(p_nb, axis=-1, keepdims=True) + jnp.sum(p_cx, axis=-1, keepdims=True)
        o = (_dot(p_nb.astype(BF16), v2) + _dot(p_cx.astype(BF16), vc_ref[:, cols])) / den
        o_ref[:, cols] = _unpair_rows(o, nq).astype(o_ref.dtype)


def _na_lat(p, pc, cq, ck, cv, cck, ccv, bias, n_batch, n_seq, n_ctx):
    rows = n_seq // GRID_W
    assert rows % NA_G == 0 and rows >= NA_SLAB
    groups = rows // NA_G
    nq = NA_G * GRID_W
    return pl.pallas_call(
        _na_lat_kernel,
        out_shape=jax.ShapeDtypeStruct((n_batch * n_seq, NA_W), BF16),
        grid=(n_batch, groups),
        in_specs=[pl.BlockSpec((nq, NA_W), lambda b, g: (b * groups + g, cq // NA_W)),
                  pl.BlockSpec((n_seq, NA_W), lambda b, g: (b, ck // NA_W)),
                  pl.BlockSpec((n_seq, NA_W), lambda b, g: (b, cv // NA_W)),
                  pl.BlockSpec((n_ctx, NA_W), lambda b, g: (b, cck // NA_W)),
                  pl.BlockSpec((n_ctx, NA_W), lambda b, g: (b, ccv // NA_W)),
                  pl.BlockSpec((NA_HEADS // 2, NA_NEG_ENTRY + 1, 2 * GRID_W, 2 * GRID_W),
                               lambda b, g: (0, 0, 0, 0), pipeline_mode=pl.Buffered(1))],
        out_specs=pl.BlockSpec((nq, NA_W), lambda b, g: (b * groups + g, 0)),
        compiler_params=_cparams("arbitrary", "arbitrary"),
        name="na_lat",
    )(p, p, p, pc, pc, bias)


def _na_ctx_kernel(q_ref, kc_ref, vc_ref, o_ref):
    n = q_ref.shape[0]
    for hp in range(NA_HEADS // 2):
        cols = slice(hp * 128, (hp + 1) * 128)
        qm = _pair_rows(q_ref[:, cols] * NA_SCALE)
        s = _dot_nt(qm, kc_ref[:, cols])
        p = jnp.exp(s - jnp.max(s, axis=-1, keepdims=True))
        o = _dot(p.astype(BF16), vc_ref[:, cols]) / jnp.sum(p, axis=-1, keepdims=True)
        o_ref[:, cols] = _unpair_rows(o, n).astype(o_ref.dtype)


def _na_ctx(pc, cq, ck, cv, n_batch, n_ctx):
    return pl.pallas_call(
        _na_ctx_kernel,
        out_shape=jax.ShapeDtypeStruct((n_batch * n_ctx, NA_W), BF16),
        grid=(n_batch,),
        in_specs=[pl.BlockSpec((n_ctx, NA_W), lambda b: (b, cq // NA_W)),
                  pl.BlockSpec((n_ctx, NA_W), lambda b: (b, ck // NA_W)),
                  pl.BlockSpec((n_ctx, NA_W), lambda b: (b, cv // NA_W))],
        out_specs=pl.BlockSpec((n_ctx, NA_W), lambda b: (b, 0)),
        compiler_params=_cparams("arbitrary"),
        name="na_ctx",
    )(pc, pc, pc)


def _na_bias_table(rpb):
    c = np.arange(GRID_W)[:, None]
    kc = np.arange(GRID_W)[None, :]
    cs = np.clip(c - NA_KW // 2, 0, GRID_W - NA_KW)
    valid = (kc >= cs) & (kc < cs + NA_KW)
    co = kc - c + (NA_KW - 1)
    onehot = (valid[None] & (co[None] == np.arange(2 * NA_KW - 1)[:, None, None])).astype(np.float32)
    t = jnp.einsum('hrj,jck->hrck', rpb.astype(F32), onehot, precision=lax.Precision.HIGHEST)
    t = t + np.where(valid, 0.0, NEG).astype(np.float32)
    t = t.reshape(NA_HEADS // 2, 2, 2 * NA_KH - 1, GRID_W, GRID_W).transpose(0, 2, 1, 3, 4)
    t = t.reshape(NA_HEADS // 2, 2 * NA_KH - 1, 2 * GRID_W, GRID_W)
    neg = jnp.full((NA_HEADS // 2, 1, 2 * GRID_W, GRID_W), NEG, F32)
    t = jnp.concatenate([neg, t, neg, neg], axis=1)
    return jnp.concatenate([t[:, :-1], t[:, 1:]], axis=-1)


def _conv_kernel(cb_ref, cc_ref, cx_ref, w_ref, o_ref):
    z = cc_ref[...].astype(F32) * cx_ref[...].astype(F32)
    n = z.shape[0]
    row = lax.broadcasted_iota(jnp.int32, z.shape, 0)
    zp = jnp.where(row == 0, 0.0, pltpu.roll(z, 1, 0))
    zn = jnp.where(row == n - 1, 0.0, pltpu.roll(z, n - 1, 0))
    w = w_ref[...]
    conv = zp * w[0:1] + z * w[1:2] + zn * w[2:3]
    o_ref[...] = (cb_ref[...].astype(F32) * conv).astype(o_ref.dtype)


def _conv(p, conv_w, n_batch, n_seq):
    tc = 256
    nb = CONV_W // tc
    return pl.pallas_call(
        _conv_kernel,
        out_shape=jax.ShapeDtypeStruct((n_batch * n_seq, CONV_W), BF16),
        grid=(n_batch, nb),
        in_specs=[pl.BlockSpec((n_seq, tc), lambda b, j: (b, C_CB // tc + j)),
                  pl.BlockSpec((n_seq, tc), lambda b, j: (b, C_CC // tc + j)),
                  pl.BlockSpec((n_seq, tc), lambda b, j: (b, C_CX // tc + j)),
                  pl.BlockSpec((CONV_K, tc), lambda b, j: (0, j))],
        out_specs=pl.BlockSpec((n_seq, tc), lambda b, j: (b, j)),
        compiler_params=_cparams("arbitrary", "arbitrary"),
        name="conv",
    )(p, p, p, conv_w)


def _fourier_kernel(x_ref, cg_ref, sg_ref, cs_ref, ss_ref, o_ref, *, norm):
    x = x_ref[...]
    a = _dot(x, cg_ref[...]).astype(BF16)
    b = _dot(x, sg_ref[...]).astype(BF16)
    o = _dot(cs_ref[...], a) - _dot(ss_ref[...], b)
    o_ref[...] = (o * norm).astype(o_ref.dtype)


@functools.lru_cache(maxsize=None)
def _dft_tables(n):
    j = np.arange(n, dtype=np.int64)
    ang = ((j[:, None] * j[None, :]) % n).astype(np.float64) * (2.0 * np.pi / n)
    return np.cos(ang).astype(np.float32), np.sin(ang).astype(np.float32)


def _fourier(p, n_batch, n_seq):
    cg, sg = (jnp.asarray(t).astype(BF16) for t in _dft_tables(FNET_GW))
    cs, ss = (jnp.asarray(t).astype(BF16) for t in _dft_tables(n_seq))
    const = lambda b, g: (0, 0)
    return pl.pallas_call(
        functools.partial(_fourier_kernel, norm=float((n_seq * FNET_GW) ** -0.5)),
        out_shape=jax.ShapeDtypeStruct((n_batch * n_seq, FNET_W), BF16),
        grid=(n_batch, FNET_GROUPS),
        in_specs=[pl.BlockSpec((n_seq, FNET_GW), lambda b, g: (b, C_FV // FNET_GW + g)),
                  pl.BlockSpec((FNET_GW, FNET_GW), const),
                  pl.BlockSpec((FNET_GW, FNET_GW), const),
                  pl.BlockSpec((n_seq, n_seq), const, pipeline_mode=pl.Buffered(1)),
                  pl.BlockSpec((n_seq, n_seq), const, pipeline_mode=pl.Buffered(1))],
        out_specs=pl.BlockSpec((n_seq, FNET_GW), lambda b, g: (b, g)),
        compiler_params=_cparams("arbitrary", "arbitrary"),
        name="fourier",
    )(p, cg, sg, cs, ss)


def _combine_kernel(o0_ref, o1_ref, o2_ref, o3_ref, g_ref, mg_ref, w_ref, m_ref):
    acc = None
    for k, o_ref in enumerate((o0_ref, o1_ref, o2_ref, o3_ref)):
        g = g_ref[:, k * 1024:(k + 1) * 1024].astype(F32)
        t = (o_ref[...].astype(F32) * _silu(g)).astype(BF16)
        gate = jax.nn.sigmoid(mg_ref[:, k * D_MODEL:(k + 1) * D_MODEL].astype(F32))
        r = _dot(t, w_ref[k]) * gate
        acc = r if acc is None else acc + r
    m_ref[...] = acc.astype(m_ref.dtype)


def _combine(p, outs, w_p, tm):
    t = p.shape[0]
    o_spec = pl.BlockSpec((tm, 1024), lambda i: (i, 0))
    return pl.pallas_call(
        _combine_kernel,
        out_shape=jax.ShapeDtypeStruct((t, D_MODEL), BF16),
        grid=(t // tm,),
        in_specs=[o_spec, o_spec, o_spec, o_spec,
                  pl.BlockSpec((tm, N_BRANCH * 1024), lambda i: (i, C_G // (N_BRANCH * 1024))),
                  pl.BlockSpec((tm, N_BRANCH * D_MODEL), lambda i: (i, C_MG // (N_BRANCH * D_MODEL))),
                  pl.BlockSpec((N_BRANCH, 1024, D_MODEL), lambda i: (0, 0, 0), pipeline_mode=pl.Buffered(1))],
        out_specs=pl.BlockSpec((tm, D_MODEL), lambda i: (i, 0)),
        compiler_params=_cparams("arbitrary"),
        name="combine",
    )(*outs, p, p, w_p)


def _outproj_kernel(m_ref, w_ref, x_ref, gt_ref, g_ref, *rest, with_next):
    y = _dot(m_ref[...], w_ref[...])
    x_new = x_ref[...] + gt_ref[0] * _rms(y, g_ref[...])
    if with_next:
        sh_ref, sc_ref, gn_ref, o_ref, u_ref = rest
        u_ref[...] = (_rms(x_new, gn_ref[...]) * (1.0 + sc_ref[0]) + sh_ref[0]).astype(BF16)
    else:
        (o_ref,) = rest
    o_ref[...] = x_new


def _outproj(m, w_out, x2, mod3, row_of_tile, g_post, tm, nxt=None):
    t = m.shape[0]
    tile = pl.BlockSpec((tm, D_MODEL), lambda i: (i, 0))
    vec = pl.BlockSpec((1, D_MODEL), lambda i: (0, 0))

    def mod_spec(part):
        return pl.BlockSpec((1, 1, D_MODEL), lambda i: (row_of_tile(i), 0, part))

    in_specs = [tile, pl.BlockSpec((D_MODEL, D_MODEL), lambda i: (0, 0), pipeline_mode=pl.Buffered(1)),
                tile, mod_spec(2), vec]
    args = [m, w_out, x2, mod3, g_post]
    out_shape = jax.ShapeDtypeStruct((t, D_MODEL), F32)
    out_specs = tile
    if nxt is not None:
        in_specs += [mod_spec(0), mod_spec(1), vec]
        args += [nxt[0], nxt[0], nxt[1]]
        out_shape = (out_shape, jax.ShapeDtypeStruct((t, D_MODEL), BF16))
        out_specs = (tile, tile)
    return pl.pallas_call(
        functools.partial(_outproj_kernel, with_next=nxt is not None),
        out_shape=out_shape,
        grid=(t // tm,),
        in_specs=in_specs,
        out_specs=out_specs,
        compiler_params=_cparams("arbitrary"),
        name="outproj",
    )(*args)


def _rope_swap(w):
    return jnp.concatenate([-w[..., 16:32], w[..., 0:16], -w[..., 48:64], w[..., 32:48]], axis=-1)


def _prep_w_uq(w):
    w3 = w.reshape(Q_LORA, MLA_HEADS, MLA_NOPE + MLA_ROPE)
    nope = w3[:, :, :MLA_NOPE].reshape(Q_LORA, MLA_W)
    r = w3[:, :, MLA_NOPE:]
    rr = jnp.concatenate([r, _rope_swap(r)], axis=-1).reshape(Q_LORA, MLA_HEADS * 128)
    return jnp.concatenate([nope, rr], axis=1).astype(BF16)


def _prep_w_ukv(w):
    w3 = w.reshape(KV_LORA, MLA_HEADS, MLA_NOPE + MLA_V)
    return jnp.concatenate([w3[:, :, :MLA_NOPE].reshape(KV_LORA, MLA_W),
                            w3[:, :, MLA_NOPE:].reshape(KV_LORA, MLA_W)], axis=1).astype(BF16)


def _rope_table(n):
    nf = MLA_ROPE // 4
    t = jnp.arange(n, dtype=jnp.int32)
    pos = jnp.stack([t // GRID_W, t % GRID_W], axis=-1).astype(F32)
    inv = ROPE_THETA ** (-jnp.arange(nf, dtype=F32) / nf)
    ang = pos[:, :, None] * inv
    cos = jnp.repeat(jnp.cos(ang), 2, axis=1).reshape(n, MLA_ROPE)
    sin = jnp.repeat(jnp.sin(ang), 2, axis=1).reshape(n, MLA_ROPE)
    return jnp.concatenate([cos, sin], axis=-1).astype(F32)


def _identity_rope_table(n):
    return jnp.concatenate([jnp.ones((n, MLA_ROPE), F32), jnp.zeros((n, MLA_ROPE), F32)], axis=-1)


def kernel(x, c, ctx, c_ctx, g_pre, g_post, w_ada, b_ada, w_in, g_q, g_kv, w_uq, w_ukv, conv_w, na_rpb,
           w_p_conv, w_p_mla, w_p_na, w_p_fnet, w_out):
    n_batch, n_seq, _ = x.shape
    n_ctx = ctx.shape[1]
    assert n_batch + 1 <= 8 and n_seq % 1024 == 0 and n_ctx % 256 == 0
    xl = x.reshape(n_batch * n_seq, D_MODEL)
    xc = ctx.reshape(n_batch * n_ctx, D_MODEL)

    cc = jnp.concatenate([c, c_ctx[None], jnp.zeros((8 - n_batch - 1, D_MODEL), c.dtype)], axis=0)
    mod = _ada(cc, w_ada, b_ada.reshape(DEPTH, 1, 3 * D_MODEL))

    wt_in = jnp.swapaxes(w_in, 1, 2)
    rope_l = _rope_table(n_seq)
    rope_c = _identity_rope_table(n_ctx)
    tm_l = min(2048, n_batch * n_seq)
    tm_c = n_batch * n_ctx
    ctx_row = lambda i: n_batch

    def lat_row_t(tm):
        return lambda i: i * tm // n_seq

    for l in range(DEPTH):
        ctx_out = l < DEPTH - 1
        mod3 = mod[l].reshape(8, 1, 3 * D_MODEL)
        w_q = _prep_w_uq(w_uq[l])
        w_kv = _prep_w_ukv(w_ukv[l])
        gq = g_q[l].reshape(1, Q_LORA)
        gkv = g_kv[l].reshape(1, KV_LORA)
        gpre = g_pre[l].reshape(1, D_MODEL)
        gpost = g_post[l].reshape(1, D_MODEL)
        w_p = jnp.stack([w_p_conv[l], w_p_mla[l], w_p_na[l], w_p_fnet[l]]).astype(BF16)
        w_o = w_out[l].astype(BF16)
        bias = _na_bias_table(na_rpb[l])

        all_tiles = tuple(range(NP_FULL // TN))
        if l == 0:
            ul = _prenorm(xl, mod3, lat_row_t(512), gpre, 512)
            uc = _prenorm(xc, mod3, ctx_row, gpre, min(512, tm_c))
        nxt = None
        if l + 1 < DEPTH:
            nxt = (mod[l + 1].reshape(8, 1, 3 * D_MODEL), g_pre[l + 1].reshape(1, D_MODEL))
        pl_ = _inproj(ul, wt_in, l, all_tiles, tm_l)
        if ctx_out:
            pc = _inproj(uc, wt_in, l, all_tiles, tm_c)
            base = 0
        else:
            pc = _inproj(uc, wt_in, l, all_tiles[KV_BASE // TN:], tm_c)
            base = KV_BASE

        kc_m, vc_m = _kvup(pc, C_KV - base, gkv, w_kv, rope_c, n_ctx, n_ctx)
        kl_m, vl_m = _kvup(pl_, C_KV, gkv, w_kv, rope_l, n_seq, 512)
        ql_m = _qup(pl_, C_QC, gq, w_q, rope_l, n_seq, 512)
        mla_l = _mla_attn(ql_m, kc_m, vc_m, kl_m, vl_m, n_batch, n_seq, n_ctx, min(2048, n_seq))
        na_l = _na_lat(pl_, pc, C_NQ, C_NK, C_NV, C_NK - base, C_NV - base, bias, n_batch, n_seq, n_ctx)
        conv_l = _conv(pl_, conv_w[l], n_batch, n_seq)
        fn_l = _fourier(pl_, n_batch, n_seq)
        m_l = _combine(pl_, (conv_l, mla_l, na_l, fn_l), w_p, 256)
        xl_new = _outproj(m_l, w_o, xl, mod3, lat_row_t(512), gpost, 512, nxt)
        if nxt is not None:
            xl_new, ul = xl_new

        if ctx_out:
            qc_m = _qup(pc, C_QC, gq, w_q, rope_c, n_ctx, n_ctx)
            mla_c = _mla_attn(qc_m, kc_m, vc_m, None, None, n_batch, n_ctx, n_ctx, n_ctx)
            na_c = _na_ctx(pc, C_NQ, C_NK, C_NV, n_batch, n_ctx)
            conv_c = _conv(pc, conv_w[l], n_batch, n_ctx)
            fn_c = _fourier(pc, n_batch, n_ctx)
            m_c = _combine(pc, (conv_c, mla_c, na_c, fn_c), w_p, 256)
            xc, uc = _outproj(m_c, w_o, xc, mod3, ctx_row, gpost, min(512, tm_c), nxt)
        xl = xl_new

    return xl.reshape(n_batch, n_seq, D_MODEL)
```

```python
import functools

import jax
import jax.numpy as jnp
import numpy as np
from jax import lax
from jax.experimental import pallas as pl
from jax.experimental.pallas import tpu as pltpu

F32 = jnp.float32
BF16 = jnp.bfloat16

D_MODEL = 2048
DEPTH = 2
GRID_W = 64
CONV_W = 1024
CONV_K = 3
MLA_HEADS = 8
MLA_NOPE = 128
MLA_ROPE = 64
MLA_V = 128
Q_LORA = 512
KV_LORA = 256
MLA_W = MLA_HEADS * MLA_V
MLA_SCALE = (MLA_NOPE + MLA_ROPE) ** -0.5
LOG2E = 1.4426950408889634
NA_HEADS = 16
NA_HD = 64
NA_W = NA_HEADS * NA_HD
NA_KH = 8
NA_KW = 16
NA_SCALE = NA_HD ** -0.5
FNET_GROUPS = 4
FNET_GW = 256
FNET_W = FNET_GROUPS * FNET_GW
N_BRANCH = 4
ROPE_THETA = 10000.0
EPS = 1e-6
NEG = -1e30

VMEM_LIMIT_BYTES = 56 * 1024 * 1024

MLA_QK = 256
KV_SEG = 512
C_MG, C_G, C_CB, C_CC, C_CX, C_FV, C_NQ, C_NK, C_NV, C_QC, C_KV = (
    0, 8192, 12288, 13312, 14336, 15360, 16384, 17408, 18432, 19456, 19968)
NP_FULL = 20480
KV_BASE = C_NK


def _cparams(*sem):
    return pltpu.CompilerParams(dimension_semantics=sem, vmem_limit_bytes=VMEM_LIMIT_BYTES)


def _rms(x, g):
    return x * lax.rsqrt(jnp.mean(x * x, axis=-1, keepdims=True) + EPS) * g


def _silu(x):
    return x * jax.nn.sigmoid(x)


def _dot(a, b):
    return jnp.dot(a, b, preferred_element_type=F32)


def _dot_nt(a, b):
    return lax.dot_general(a, b, (((1,), (1,)), ((), ())), preferred_element_type=F32)


def _ada_kernel(c_ref, w_ref, b_ref, o_ref):
    a = _silu(c_ref[...]).astype(BF16)
    o_ref[0] = _dot(a, w_ref[0].astype(BF16)) + b_ref[0]


def _ada(cc, w_ada, b_ada):
    tn = 1024
    return pl.pallas_call(
        _ada_kernel,
        out_shape=jax.ShapeDtypeStruct((DEPTH, 8, 3 * D_MODEL), F32),
        grid=(DEPTH, 3 * D_MODEL // tn),
        in_specs=[pl.BlockSpec((8, D_MODEL), lambda l, j: (0, 0)),
                  pl.BlockSpec((1, D_MODEL, tn), lambda l, j: (l, 0, j)),
                  pl.BlockSpec((1, 1, tn), lambda l, j: (l, 0, j))],
        out_specs=pl.BlockSpec((1, 8, tn), lambda l, j: (l, 0, j)),
        compiler_params=_cparams("arbitrary", "arbitrary"),
        name="ada",
    )(cc, w_ada, b_ada)


def _prenorm_kernel(x_ref, sh_ref, sc_ref, g_ref, u_ref):
    y = _rms(x_ref[...], g_ref[...])
    u_ref[...] = (y * (1.0 + sc_ref[0]) + sh_ref[0]).astype(BF16)


def _prenorm(x2, mod3, row_of_tile, g_pre, tm):
    t = x2.shape[0]
    return pl.pallas_call(
        _prenorm_kernel,
        out_shape=jax.ShapeDtypeStruct((t, D_MODEL), BF16),
        grid=(t // tm,),
        in_specs=[pl.BlockSpec((tm, D_MODEL), lambda i: (i, 0)),
                  pl.BlockSpec((1, 1, D_MODEL), lambda i: (row_of_tile(i), 0, 0)),
                  pl.BlockSpec((1, 1, D_MODEL), lambda i: (row_of_tile(i), 0, 1)),
                  pl.BlockSpec((1, D_MODEL), lambda i: (0, 0))],
        out_specs=pl.BlockSpec((tm, D_MODEL), lambda i: (i, 0)),
        compiler_params=_cparams("arbitrary"),
        name="prenorm",
    )(x2, mod3, mod3, g_pre)


TN = 1024
W_ROWS = 256
N_ORIG = 20288

_TILE_SRC = ([12096 + TN * t for t in range(8)] + [8000 + TN * t for t in range(4)]
             + [3904 + TN * t for t in range(3)] + [6976, 2880, 320, 1344, -1])
SRC_QC = 2368
KV_ROWS = KV_LORA + MLA_ROPE


def _inproj_kernel(src_ref, u_ref, wt_hbm, o_ref, stage, wbuf, sems, *, layer, n_tiles):
    j = pl.program_id(0)

    def main_copy(jj):
        start = pl.multiple_of(src_ref[jj], 8)
        return pltpu.make_async_copy(wt_hbm.at[layer, pl.ds(start, TN), :], stage, sems.at[0])

    def tail_copies():
        return (pltpu.make_async_copy(wt_hbm.at[layer, pl.ds(SRC_QC, Q_LORA), :],
                                      stage.at[pl.ds(0, Q_LORA), :], sems.at[0]),
                pltpu.make_async_copy(wt_hbm.at[layer, pl.ds(0, KV_ROWS), :],
                                      stage.at[pl.ds(Q_LORA, KV_ROWS), :], sems.at[1]))

    def start(jj):
        @pl.when(src_ref[jj] >= 0)
        def _():
            main_copy(jj).start()

        @pl.when(src_ref[jj] < 0)
        def _():
            for c in tail_copies():
                c.start()

    def convert(n_rows):
        def body(r, carry):
            r0 = pl.multiple_of(r * W_ROWS, W_ROWS)
            wbuf[pl.ds(r0, W_ROWS), :] = stage[pl.ds(r0, W_ROWS), :].astype(BF16)
            return carry

        lax.fori_loop(0, n_rows // W_ROWS, body, 0)

    @pl.when(pl.program_id(1) == 0)
    def _():
        @pl.when(j == 0)
        def _():
            start(0)

        @pl.when(src_ref[j] >= 0)
        def _():
            main_copy(j).wait()
            convert(TN)

        @pl.when(src_ref[j] < 0)
        def _():
            for c in tail_copies():
                c.wait()
            convert(768)
            kr0 = Q_LORA + KV_LORA
            wbuf[768:832, :] = stage[768:832, :].astype(BF16)
            wbuf[832:848, :] = (-stage[kr0 + 16:kr0 + 32, :]).astype(BF16)
            wbuf[848:864, :] = stage[kr0:kr0 + 16, :].astype(BF16)
            wbuf[864:880, :] = (-stage[kr0 + 48:kr0 + 64, :]).astype(BF16)
            wbuf[880:896, :] = stage[kr0 + 32:kr0 + 48, :].astype(BF16)
            wbuf[896:TN, :] = jnp.zeros((TN - 896, D_MODEL), BF16)

        @pl.when(j + 1 < n_tiles)
        def _():
            start(j + 1)

    o_ref[...] = _dot_nt(u_ref[...], wbuf[...]).astype(o_ref.dtype)


def _inproj(u, wt_in, layer, tiles, tm):
    t = u.shape[0]
    src = np.array([_TILE_SRC[j] for j in tiles], np.int32)
    assert all(s % 8 == 0 and s + TN <= N_ORIG for s in src[src >= 0])
    grid_spec = pltpu.PrefetchScalarGridSpec(
        num_scalar_prefetch=1,
        grid=(len(tiles), t // tm),
        in_specs=[pl.BlockSpec((tm, D_MODEL), lambda j, i, s: (i, 0)),
                  pl.BlockSpec(memory_space=pl.ANY)],
        out_specs=pl.BlockSpec((tm, TN), lambda j, i, s: (i, j)),
        scratch_shapes=[pltpu.VMEM((TN, D_MODEL), F32), pltpu.VMEM((TN, D_MODEL), BF16),
                        pltpu.SemaphoreType.DMA((2,))])
    return pl.pallas_call(
        functools.partial(_inproj_kernel, layer=layer, n_tiles=len(tiles)),
        out_shape=jax.ShapeDtypeStruct((t, len(tiles) * TN), BF16),
        grid_spec=grid_spec,
        compiler_params=_cparams("arbitrary", "arbitrary"),
        name="inproj",
    )(jnp.asarray(src), u, wt_in)


def _qup_kernel(h_ref, g_ref, w_ref, cs_ref, o_ref):
    hn = _rms(h_ref[...].astype(F32), g_ref[...]).astype(BF16)
    qq = _dot(hn, w_ref[...]) * (MLA_SCALE * LOG2E)
    cs = cs_ref[...]
    for h in range(MLA_HEADS):
        o_ref[:, h * MLA_QK:h * MLA_QK + MLA_NOPE] = qq[:, h * 128:(h + 1) * 128].astype(BF16)
        t = qq[:, MLA_W + h * 128:MLA_W + (h + 1) * 128] * cs
        o_ref[:, h * MLA_QK + MLA_NOPE:(h + 1) * MLA_QK] = (t + pltpu.roll(t, 64, 1)).astype(BF16)


def _qup(p, col, g_q, w_q, cs, n_seq, tm):
    t = p.shape[0]
    spt = n_seq // tm
    return pl.pallas_call(
        _qup_kernel,
        out_shape=jax.ShapeDtypeStruct((t, MLA_HEADS * MLA_QK), BF16),
        grid=(t // tm,),
        in_specs=[pl.BlockSpec((tm, Q_LORA), lambda i: (i, col // Q_LORA)),
                  pl.BlockSpec((1, Q_LORA), lambda i: (0, 0)),
                  pl.BlockSpec((Q_LORA, 2 * MLA_W), lambda i: (0, 0)),
                  pl.BlockSpec((tm, 128), lambda i: (i % spt, 0))],
        out_specs=pl.BlockSpec((tm, MLA_HEADS * MLA_QK), lambda i: (i, 0)),
        compiler_params=_cparams("arbitrary"),
        name="mla_qup",
    )(p, g_q, w_q, cs)


def _kvup_kernel(p_ref, g_ref, w_ref, cs_ref, k_ref, v_ref):
    p = p_ref[...]
    cn = _rms(p[:, :KV_LORA].astype(F32), g_ref[...]).astype(BF16)
    kv = _dot(cn, w_ref[...])
    t = p[:, KV_LORA:KV_LORA + 128].astype(F32) * cs_ref[...]
    lane = lax.broadcasted_iota(jnp.int32, t.shape, 1)
    kr = jnp.where(lane < MLA_ROPE, t + pltpu.roll(t, 64, 1), 0.0).astype(BF16)
    for h in range(MLA_HEADS):
        k_ref[:, h * MLA_QK:h * MLA_QK + MLA_NOPE] = kv[:, h * 128:(h + 1) * 128].astype(BF16)
        k_ref[:, h * MLA_QK + MLA_NOPE:(h + 1) * MLA_QK] = kr
    v_ref[...] = kv[:, MLA_W:].astype(BF16)


def _kvup(p, col, g_kv, w_kv, cs, n_seq, tm):
    t = p.shape[0]
    spt = n_seq // tm
    return pl.pallas_call(
        _kvup_kernel,
        out_shape=(jax.ShapeDtypeStruct((t, MLA_HEADS * MLA_QK), BF16),
                   jax.ShapeDtypeStruct((t, MLA_W), BF16)),
        grid=(t // tm,),
        in_specs=[pl.BlockSpec((tm, KV_SEG), lambda i: (i, col // KV_SEG)),
                  pl.BlockSpec((1, KV_LORA), lambda i: (0, 0)),
                  pl.BlockSpec((KV_LORA, 2 * MLA_W), lambda i: (0, 0)),
                  pl.BlockSpec((tm, 128), lambda i: (i % spt, 0))],
        out_specs=(pl.BlockSpec((tm, MLA_HEADS * MLA_QK), lambda i: (i, 0)),
                   pl.BlockSpec((tm, MLA_W), lambda i: (i, 0))),
        compiler_params=_cparams("arbitrary"),
        name="mla_kvup",
    )(p, g_kv, w_kv, cs)


MLA_ROWS = 256


def _mla_attn_kernel(q_ref, kc_ref, vc_ref, *rest, has_lat):
    if has_lat:
        kl_ref, vl_ref, o_ref = rest
    else:
        (o_ref,) = rest
    for r0 in range(0, q_ref.shape[0], MLA_ROWS):
        rows = slice(r0, r0 + MLA_ROWS)
        q = q_ref[rows, :]
        s_c = _dot_nt(q, kc_ref[...])
        m = jnp.max(s_c, axis=-1, keepdims=True)
        if has_lat:
            s_l = _dot_nt(q, kl_ref[...])
            m = jnp.maximum(m, jnp.max(s_l, axis=-1, keepdims=True))
            p_l = jnp.exp2(s_l - m)
        p_c = jnp.exp2(s_c - m)
        den = jnp.sum(p_c, axis=-1, keepdims=True)
        o = _dot(p_c.astype(BF16), vc_ref[...])
        if has_lat:
            den = den + jnp.sum(p_l, axis=-1, keepdims=True)
            o = o + _dot(p_l.astype(BF16), vl_ref[...])
        o_ref[rows, :] = (o / den).astype(o_ref.dtype)


def _mla_attn(q, kc, vc, kl, vl, n_batch, n_q, n_ctx, tq):
    has_lat = kl is not None
    qt = n_q // tq
    in_specs = [pl.BlockSpec((tq, MLA_QK), lambda b, h, i: (b * qt + i, h)),
                pl.BlockSpec((n_ctx, MLA_QK), lambda b, h, i: (b, h)),
                pl.BlockSpec((n_ctx, MLA_V), lambda b, h, i: (b, h))]
    args = [q, kc, vc]
    if has_lat:
        in_specs += [pl.BlockSpec((n_q, MLA_QK), lambda b, h, i: (b, h)),
                     pl.BlockSpec((n_q, MLA_V), lambda b, h, i: (b, h))]
        args += [kl, vl]
    return pl.pallas_call(
        functools.partial(_mla_attn_kernel, has_lat=has_lat),
        out_shape=jax.ShapeDtypeStruct((n_batch * n_q, MLA_W), BF16),
        grid=(n_batch, MLA_HEADS, qt),
        in_specs=in_specs,
        out_specs=pl.BlockSpec((tq, MLA_V), lambda b, h, i: (b * qt + i, h)),
        compiler_params=_cparams("arbitrary", "arbitrary", "arbitrary"),
        name="mla_attn_lat" if has_lat else "mla_attn_ctx",
    )(*args)


def _pair_rows(q2):
    lane = lax.broadcasted_iota(jnp.int32, q2.shape, 1)
    zero = jnp.zeros_like(q2)
    return jnp.concatenate([jnp.where(lane < NA_HD, q2, zero), jnp.where(lane >= NA_HD, q2, zero)], axis=0)


def _unpair_rows(o, n):
    lane = lax.broadcasted_iota(jnp.int32, (n, 128), 1)
    return jnp.where(lane < NA_HD, o[:n], o[n:])


NA_G = 4
NA_SLAB = NA_G + NA_KH
NA_NEG_ENTRY = 2 * NA_KH


def _na_lat_kernel(q_ref, k_ref, v_ref, kc_ref, vc_ref, b_ref, o_ref):
    g = pl.program_id(1)
    rows = k_ref.shape[0] // GRID_W
    nq = NA_G * GRID_W
    us = jnp.clip(NA_G * g - NA_KH // 2, 0, rows - NA_SLAB)
    start = pl.multiple_of(us * GRID_W, GRID_W)

    entry, ok_l, ok_r = {}, {}, {}
    for dr in range(NA_G):
        r = NA_G * g + dr
        rs = jnp.clip(r - NA_KH // 2, 0, rows - NA_KH)
        for p in range(NA_SLAB // 2):
            kr = us + 2 * p
            l_ok = jnp.logical_and(kr >= rs, kr < rs + NA_KH)
            r_ok = jnp.logical_and(kr + 1 >= rs, kr + 1 < rs + NA_KH)
            ro = kr - r + (NA_KH - 1)
            entry[dr, p] = jnp.where(jnp.logical_or(l_ok, r_ok), jnp.clip(ro + 1, 0, NA_NEG_ENTRY - 1),
                                     NA_NEG_ENTRY)
            ok_l[dr, p] = l_ok.astype(jnp.int32)
            ok_r[dr, p] = r_ok.astype(jnp.int32)
    lane = lax.broadcasted_iota(jnp.int32, (GRID_W, 128), 1)

    def bias_tile(hp, sub, dr, p):
        t = b_ref[hp, entry[dr, p], sub * GRID_W:(sub + 1) * GRID_W, :]
        if dr % 2 == 1 and p in ((dr - 1) // 2, (dr + NA_KH - 1) // 2):
            t = jnp.where(jnp.where(lane < GRID_W, ok_l[dr, p], ok_r[dr, p]) > 0, t, NEG)
        return t

    for hp in range(NA_HEADS // 2):
        cols = slice(hp * 128, (hp + 1) * 128)
        qm = _pair_rows(q_ref[:, cols] * NA_SCALE)
        k2 = k_ref[pl.ds(start, NA_SLAB * GRID_W), cols]
        v2 = v_ref[pl.ds(start, NA_SLAB * GRID_W), cols]
        bias = jnp.concatenate(
            [jnp.concatenate([bias_tile(hp, sub, dr, p) for p in range(NA_SLAB // 2)], axis=1)
             for sub in range(2) for dr in range(NA_G)], axis=0)
        s_nb = _dot_nt(qm, k2) + bias
        s_cx = _dot_nt(qm, kc_ref[:, cols])
        m = jnp.maximum(jnp.max(s_nb, axis=-1, keepdims=True), jnp.max(s_cx, axis=-1, keepdims=True))
        p_nb = jnp.exp(s_nb - m)
        p_cx = jnp.exp(s_cx - m)
        den = jnp.sum(p_nb, axis=-1, keepdims=True) + jnp.sum(p_cx, axis=-1, keepdims=True)
        o = (_dot(p_nb.astype(BF16), v2) + _dot(p_cx.astype(BF16), vc_ref[:, cols])) / den
        o_ref[:, cols] = _unpair_rows(o, nq).astype(o_ref.dtype)


def _na_lat(p, pc, cq, ck, cv, cck, ccv, bias, n_batch, n_seq, n_ctx):
    rows = n_seq // GRID_W
    assert rows % NA_G == 0 and rows >= NA_SLAB
    groups = rows // NA_G
    nq = NA_G * GRID_W
    return pl.pallas_call(
        _na_lat_kernel,
        out_shape=jax.ShapeDtypeStruct((n_batch * n_seq, NA_W), BF16),
        grid=(n_batch, groups),
        in_specs=[pl.BlockSpec((nq, NA_W), lambda b, g: (b * groups + g, cq // NA_W)),
                  pl.BlockSpec((n_seq, NA_W), lambda b, g: (b, ck // NA_W)),
                  pl.BlockSpec((n_seq, NA_W), lambda b, g: (b, cv // NA_W)),
                  pl.BlockSpec((n_ctx, NA_W), lambda b, g: (b, cck // NA_W)),
                  pl.BlockSpec((n_ctx, NA_W), lambda b, g: (b, ccv // NA_W)),
                  pl.BlockSpec((NA_HEADS // 2, NA_NEG_ENTRY + 1, 2 * GRID_W, 2 * GRID_W),
                               lambda b, g: (0, 0, 0, 0), pipeline_mode=pl.Buffered(1))],
        out_specs=pl.BlockSpec((nq, NA_W), lambda b, g: (b * groups + g, 0)),
        compiler_params=_cparams("arbitrary", "arbitrary"),
        name="na_lat",
    )(p, p, p, pc, pc, bias)


def _na_ctx_kernel(q_ref, kc_ref, vc_ref, o_ref):
    n = q_ref.shape[0]
    for hp in range(NA_HEADS // 2):
        cols = slice(hp * 128, (hp + 1) * 128)
        qm = _pair_rows(q_ref[:, cols] * NA_SCALE)
        s = _dot_nt(qm, kc_ref[:, cols])
        p = jnp.exp(s - jnp.max(s, axis=-1, keepdims=True))
        o = _dot(p.astype(BF16), vc_ref[:, cols]) / jnp.sum(p, axis=-1, keepdims=True)
        o_ref[:, cols] = _unpair_rows(o, n).astype(o_ref.dtype)


def _na_ctx(pc, cq, ck, cv, n_batch, n_ctx):
    return pl.pallas_call(
        _na_ctx_kernel,
        out_shape=jax.ShapeDtypeStruct((n_batch * n_ctx, NA_W), BF16),
        grid=(n_batch,),
        in_specs=[pl.BlockSpec((n_ctx, NA_W), lambda b: (b, cq // NA_W)),
                  pl.BlockSpec((n_ctx, NA_W), lambda b: (b, ck // NA_W)),
                  pl.BlockSpec((n_ctx, NA_W), lambda b: (b, cv // NA_W))],
        out_specs=pl.BlockSpec((n_ctx, NA_W), lambda b: (b, 0)),
        compiler_params=_cparams("arbitrary"),
        name="na_ctx",
    )(pc, pc, pc)


def _na_bias_table(rpb):
    c = np.arange(GRID_W)[:, None]
    kc = np.arange(GRID_W)[None, :]
    cs = np.clip(c - NA_KW // 2, 0, GRID_W - NA_KW)
    valid = (kc >= cs) & (kc < cs + NA_KW)
    co = kc - c + (NA_KW - 1)
    onehot = (valid[None] & (co[None] == np.arange(2 * NA_KW - 1)[:, None, None])).astype(np.float32)
    t = jnp.einsum('hrj,jck->hrck', rpb.astype(F32), onehot, precision=lax.Precision.HIGHEST)
    t = t + np.where(valid, 0.0, NEG).astype(np.float32)
    t = t.reshape(NA_HEADS // 2, 2, 2 * NA_KH - 1, GRID_W, GRID_W).transpose(0, 2, 1, 3, 4)
    t = t.reshape(NA_HEADS // 2, 2 * NA_KH - 1, 2 * GRID_W, GRID_W)
    neg = jnp.full((NA_HEADS // 2, 1, 2 * GRID_W, GRID_W), NEG, F32)
    t = jnp.concatenate([neg, t, neg, neg], axis=1)
    return jnp.concatenate([t[:, :-1], t[:, 1:]], axis=-1)


def _conv_kernel(cb_ref, cc_ref, cx_ref, w_ref, o_ref):
    z = cc_ref[...].astype(F32) * cx_ref[...].astype(F32)
    n = z.shape[0]
    row = lax.broadcasted_iota(jnp.int32, z.shape, 0)
    zp = jnp.where(row == 0, 0.0, pltpu.roll(z, 1, 0))
    zn = jnp.where(row == n - 1, 0.0, pltpu.roll(z, n - 1, 0))
    w = w_ref[...]
    conv = zp * w[0:1] + z * w[1:2] + zn * w[2:3]
    o_ref[...] = (cb_ref[...].astype(F32) * conv).astype(o_ref.dtype)


def _conv(p, conv_w, n_batch, n_seq):
    tc = 256
    nb = CONV_W // tc
    return pl.pallas_call(
        _conv_kernel,
        out_shape=jax.ShapeDtypeStruct((n_batch * n_seq, CONV_W), BF16),
        grid=(n_batch, nb),
        in_specs=[pl.BlockSpec((n_seq, tc), lambda b, j: (b, C_CB // tc + j)),
                  pl.BlockSpec((n_seq, tc), lambda b, j: (b, C_CC // tc + j)),
                  pl.BlockSpec((n_seq, tc), lambda b, j: (b, C_CX // tc + j)),
                  pl.BlockSpec((CONV_K, tc), lambda b, j: (0, j))],
        out_specs=pl.BlockSpec((n_seq, tc), lambda b, j: (b, j)),
        compiler_params=_cparams("arbitrary", "arbitrary"),
        name="conv",
    )(p, p, p, conv_w)


def _fourier_kernel(x_ref, cg_ref, sg_ref, ch_ref, sh_ref, j_ref, o_ref, *, norm):
    x = x_ref[...]
    n = x.shape[0]
    h = n // 2
    rb = j_ref.shape[0]
    jm = j_ref[...]

    def rev(t):
        m = t.shape[0]
        return jnp.concatenate([_dot(jm, t[m - (i + 1) * rb:m - i * rb, :]) for i in range(m // rb)], axis=0)

    def mirror(t, first_row):
        row = lax.broadcasted_iota(jnp.int32, t.shape, 0)
        return jnp.where(row == 0, first_row, pltpu.roll(rev(t), 1, 0))

    a = _dot(x, cg_ref[...])
    b = _dot(x, sg_ref[...])
    ta = mirror(a.astype(BF16), 0.0)
    tb = mirror(b.astype(BF16), 0.0)
    ae = (a + ta)[:h].astype(BF16)
    bo = (b - tb)[:h].astype(BF16)
    rowh = lax.broadcasted_iota(jnp.int32, (h, x.shape[1]), 0)
    sign_h = jnp.where(rowh % 2 == 0, 1.0, -1.0)
    e = _dot(ch_ref[...], ae) + sign_h * a[h:h + 1, :]
    o = _dot(sh_ref[...], bo)
    o_ref[:h, :] = ((e - o) * norm).astype(o_ref.dtype)
    rown = lax.broadcasted_iota(jnp.int32, a.shape, 0)
    mid = jnp.sum(jnp.where(rown % 2 == 0, a, -a), axis=0, keepdims=True)
    g = ((e + o) * norm).astype(BF16)
    o_ref[h:, :] = mirror(g, mid * norm).astype(o_ref.dtype)


@functools.lru_cache(maxsize=None)
def _dft_tables(n):
    j = np.arange(n, dtype=np.int64)
    ang = ((j[:, None] * j[None, :]) % n).astype(np.float64) * (2.0 * np.pi / n)
    return np.cos(ang).astype(np.float32), np.sin(ang).astype(np.float32)


def _fourier(p, n_batch, n_seq):
    h = n_seq // 2
    rb = min(256, h)
    assert h % rb == 0
    cg, sg = (jnp.asarray(t).astype(BF16) for t in _dft_tables(FNET_GW))
    ch, sh = (jnp.asarray(np.ascontiguousarray(t[:h, :h])).astype(BF16) for t in _dft_tables(n_seq))
    flip = jnp.asarray(np.eye(rb, dtype=np.float32)[::-1].copy()).astype(BF16)
    const = lambda b, g: (0, 0)
    return pl.pallas_call(
        functools.partial(_fourier_kernel, norm=float((n_seq * FNET_GW) ** -0.5)),
        out_shape=jax.ShapeDtypeStruct((n_batch * n_seq, FNET_W), BF16),
        grid=(n_batch, FNET_GROUPS),
        in_specs=[pl.BlockSpec((n_seq, FNET_GW), lambda b, g: (b, C_FV // FNET_GW + g)),
                  pl.BlockSpec((FNET_GW, FNET_GW), const),
                  pl.BlockSpec((FNET_GW, FNET_GW), const),
                  pl.BlockSpec((h, h), const, pipeline_mode=pl.Buffered(1)),
                  pl.BlockSpec((h, h), const, pipeline_mode=pl.Buffered(1)),
                  pl.BlockSpec((rb, rb), const)],
        out_specs=pl.BlockSpec((n_seq, FNET_GW), lambda b, g: (b, g)),
        compiler_params=_cparams("arbitrary", "arbitrary"),
        name="fourier",
    )(p, cg, sg, ch, sh, flip)


def _combine_kernel(o0_ref, o1_ref, o2_ref, o3_ref, g_ref, mg_ref, w_ref, m_ref):
    acc = None
    for k, o_ref in enumerate((o0_ref, o1_ref, o2_ref, o3_ref)):
        g = g_ref[:, k * 1024:(k + 1) * 1024].astype(F32)
        t = (o_ref[...].astype(F32) * _silu(g)).astype(BF16)
        gate = jax.nn.sigmoid(mg_ref[:, k * D_MODEL:(k + 1) * D_MODEL].astype(F32))
        r = _dot(t, w_ref[k]) * gate
        acc = r if acc is None else acc + r
    m_ref[...] = acc.astype(m_ref.dtype)


def _combine(p, outs, w_p, tm):
    t = p.shape[0]
    o_spec = pl.BlockSpec((tm, 1024), lambda i: (i, 0))
    return pl.pallas_call(
        _combine_kernel,
        out_shape=jax.ShapeDtypeStruct((t, D_MODEL), BF16),
        grid=(t // tm,),
        in_specs=[o_spec, o_spec, o_spec, o_spec,
                  pl.BlockSpec((tm, N_BRANCH * 1024), lambda i: (i, C_G // (N_BRANCH * 1024))),
                  pl.BlockSpec((tm, N_BRANCH * D_MODEL), lambda i: (i, C_MG // (N_BRANCH * D_MODEL))),
                  pl.BlockSpec((N_BRANCH, 1024, D_MODEL), lambda i: (0, 0, 0), pipeline_mode=pl.Buffered(1))],
        out_specs=pl.BlockSpec((tm, D_MODEL), lambda i: (i, 0)),
        compiler_params=_cparams("arbitrary"),
        name="combine",
    )(*outs, p, p, w_p)


OUT_ROWS = 256


def _outproj_kernel(m_ref, w_ref, x_ref, gt_ref, g_ref, *rest, with_next):
    if with_next:
        sh_ref, sc_ref, gn_ref, o_ref, u_ref = rest
    else:
        (o_ref,) = rest
    for r0 in range(0, m_ref.shape[0], OUT_ROWS):
        rows = slice(r0, r0 + OUT_ROWS)
        y = _dot(m_ref[rows, :], w_ref[...])
        x_new = x_ref[rows, :] + gt_ref[0] * _rms(y, g_ref[...])
        if with_next:
            u_ref[rows, :] = (_rms(x_new, gn_ref[...]) * (1.0 + sc_ref[0]) + sh_ref[0]).astype(BF16)
        o_ref[rows, :] = x_new


def _outproj(m, w_out, x2, mod3, row_of_tile, g_post, tm, nxt=None):
    t = m.shape[0]
    tile = pl.BlockSpec((tm, D_MODEL), lambda i: (i, 0))
    vec = pl.BlockSpec((1, D_MODEL), lambda i: (0, 0))

    def mod_spec(part):
        return pl.BlockSpec((1, 1, D_MODEL), lambda i: (row_of_tile(i), 0, part))

    in_specs = [tile, pl.BlockSpec((D_MODEL, D_MODEL), lambda i: (0, 0), pipeline_mode=pl.Buffered(1)),
                tile, mod_spec(2), vec]
    args = [m, w_out, x2, mod3, g_post]
    out_shape = jax.ShapeDtypeStruct((t, D_MODEL), F32)
    out_specs = tile
    if nxt is not None:
        in_specs += [mod_spec(0), mod_spec(1), vec]
        args += [nxt[0], nxt[0], nxt[1]]
        out_shape = (out_shape, jax.ShapeDtypeStruct((t, D_MODEL), BF16))
        out_specs = (tile, tile)
    return pl.pallas_call(
        functools.partial(_outproj_kernel, with_next=nxt is not None),
        out_shape=out_shape,
        grid=(t // tm,),
        in_specs=in_specs,
        out_specs=out_specs,
        compiler_params=_cparams("arbitrary"),
        name="outproj",
    )(*args)


def _rope_swap(w):
    return jnp.concatenate([-w[..., 16:32], w[..., 0:16], -w[..., 48:64], w[..., 32:48]], axis=-1)


def _prep_w_uq(w):
    w3 = w.reshape(Q_LORA, MLA_HEADS, MLA_NOPE + MLA_ROPE)
    nope = w3[:, :, :MLA_NOPE].reshape(Q_LORA, MLA_W)
    r = w3[:, :, MLA_NOPE:]
    rr = jnp.concatenate([r, _rope_swap(r)], axis=-1).reshape(Q_LORA, MLA_HEADS * 128)
    return jnp.concatenate([nope, rr], axis=1).astype(BF16)


def _prep_w_ukv(w):
    w3 = w.reshape(KV_LORA, MLA_HEADS, MLA_NOPE + MLA_V)
    return jnp.concatenate([w3[:, :, :MLA_NOPE].reshape(KV_LORA, MLA_W),
                            w3[:, :, MLA_NOPE:].reshape(KV_LORA, MLA_W)], axis=1).astype(BF16)


def _rope_table(n):
    nf = MLA_ROPE // 4
    t = jnp.arange(n, dtype=jnp.int32)
    pos = jnp.stack([t // GRID_W, t % GRID_W], axis=-1).astype(F32)
    inv = ROPE_THETA ** (-jnp.arange(nf, dtype=F32) / nf)
    ang = pos[:, :, None] * inv
    cos = jnp.repeat(jnp.cos(ang), 2, axis=1).reshape(n, MLA_ROPE)
    sin = jnp.repeat(jnp.sin(ang), 2, axis=1).reshape(n, MLA_ROPE)
    return jnp.concatenate([cos, sin], axis=-1).astype(F32)


def _identity_rope_table(n):
    return jnp.concatenate([jnp.ones((n, MLA_ROPE), F32), jnp.zeros((n, MLA_ROPE), F32)], axis=-1)


def kernel(x, c, ctx, c_ctx, g_pre, g_post, w_ada, b_ada, w_in, g_q, g_kv, w_uq, w_ukv, conv_w, na_rpb,
           w_p_conv, w_p_mla, w_p_na, w_p_fnet, w_out):
    n_batch, n_seq, _ = x.shape
    n_ctx = ctx.shape[1]
    assert n_batch + 1 <= 8 and n_seq % 1024 == 0 and n_ctx % 256 == 0
    xl = x.reshape(n_batch * n_seq, D_MODEL)
    xc = ctx.reshape(n_batch * n_ctx, D_MODEL)

    cc = jnp.concatenate([c, c_ctx[None], jnp.zeros((8 - n_batch - 1, D_MODEL), c.dtype)], axis=0)
    mod = _ada(cc, w_ada, b_ada.reshape(DEPTH, 1, 3 * D_MODEL))

    wt_in = jnp.swapaxes(w_in, 1, 2)
    rope_l = _rope_table(n_seq)
    rope_c = _identity_rope_table(n_ctx)
    tm_l = min(2048, n_batch * n_seq)
    tm_c = n_batch * n_ctx
    ctx_row = lambda i: n_batch

    def lat_row_t(tm):
        return lambda i: i * tm // n_seq

    for l in range(DEPTH):
        ctx_out = l < DEPTH - 1
        mod3 = mod[l].reshape(8, 1, 3 * D_MODEL)
        w_q = _prep_w_uq(w_uq[l])
        w_kv = _prep_w_ukv(w_ukv[l])
        gq = g_q[l].reshape(1, Q_LORA)
        gkv = g_kv[l].reshape(1, KV_LORA)
        gpre = g_pre[l].reshape(1, D_MODEL)
        gpost = g_post[l].reshape(1, D_MODEL)
        w_p = jnp.stack([w_p_conv[l], w_p_mla[l], w_p_na[l], w_p_fnet[l]]).astype(BF16)
        w_o = w_out[l].astype(BF16)
        bias = _na_bias_table(na_rpb[l])

        all_tiles = tuple(range(NP_FULL // TN))
        if l == 0:
            ul = _prenorm(xl, mod3, lat_row_t(512), gpre, 512)
            uc = _prenorm(xc, mod3, ctx_row, gpre, min(512, tm_c))
        nxt = None
        if l + 1 < DEPTH:
            nxt = (mod[l + 1].reshape(8, 1, 3 * D_MODEL), g_pre[l + 1].reshape(1, D_MODEL))
        pl_ = _inproj(ul, wt_in, l, all_tiles, tm_l)
        if ctx_out:
            pc = _inproj(uc, wt_in, l, all_tiles, tm_c)
            base = 0
        else:
            pc = _inproj(uc, wt_in, l, all_tiles[KV_BASE // TN:], tm_c)
            base = KV_BASE

        kc_m, vc_m = _kvup(pc, C_KV - base, gkv, w_kv, rope_c, n_ctx, n_ctx)
        kl_m, vl_m = _kvup(pl_, C_KV, gkv, w_kv, rope_l, n_seq, 512)
        ql_m = _qup(pl_, C_QC, gq, w_q, rope_l, n_seq, 512)
        mla_l = _mla_attn(ql_m, kc_m, vc_m, kl_m, vl_m, n_batch, n_seq, n_ctx, min(2048, n_seq))
        na_l = _na_lat(pl_, pc, C_NQ, C_NK, C_NV, C_NK - base, C_NV - base, bias, n_batch, n_seq, n_ctx)
        conv_l = _conv(pl_, conv_w[l], n_batch, n_seq)
        fn_l = _fourier(pl_, n_batch, n_seq)
        m_l = _combine(pl_, (conv_l, mla_l, na_l, fn_l), w_p, 256)
        xl_new = _outproj(m_l, w_o, xl, mod3, lat_row_t(512), gpost, 512, nxt)
        if nxt is not None:
            xl_new, ul = xl_new

        if ctx_out:
            qc_m = _qup(pc, C_QC, gq, w_q, rope_c, n_ctx, n_ctx)
            mla_c = _mla_attn(qc_m, kc_m, vc_m, None, None, n_batch, n_ctx, n_ctx, n_ctx)
            na_c = _na_ctx(pc, C_NQ, C_NK, C_NV, n_batch, n_ctx)
            conv_c = _conv(pc, conv_w[l], n_batch, n_ctx)
            fn_c = _fourier(pc, n_batch, n_ctx)
            m_c = _combine(pc, (conv_c, mla_c, na_c, fn_c), w_p, 256)
            xc, uc = _outproj(m_c, w_o, xc, mod3, ctx_row, gpost, min(512, tm_c), nxt)
        xl = xl_new

    return xl.reshape(n_batch, n_seq, D_MODEL)
```

```python
import functools

import jax
import jax.numpy as jnp
import numpy as np
from jax import lax
from jax.experimental import pallas as pl
from jax.experimental.pallas import tpu as pltpu

F32 = jnp.float32
BF16 = jnp.bfloat16

D_MODEL = 2048
DEPTH = 2
GRID_W = 64
CONV_W = 1024
CONV_K = 3
MLA_HEADS = 8
MLA_NOPE = 128
MLA_ROPE = 64
MLA_V = 128
Q_LORA = 512
KV_LORA = 256
MLA_W = MLA_HEADS * MLA_V
MLA_SCALE = (MLA_NOPE + MLA_ROPE) ** -0.5
LOG2E = 1.4426950408889634
NA_HEADS = 16
NA_HD = 64
NA_W = NA_HEADS * NA_HD
NA_KH = 8
NA_KW = 16
NA_SCALE = NA_HD ** -0.5
FNET_GROUPS = 4
FNET_GW = 256
FNET_W = FNET_GROUPS * FNET_GW
N_BRANCH = 4
ROPE_THETA = 10000.0
EPS = 1e-6
NEG = -1e30

VMEM_LIMIT_BYTES = 56 * 1024 * 1024

MLA_QK = 256
KV_SEG = 512
C_MG, C_G, C_CB, C_CC, C_CX, C_FV, C_NQ, C_NK, C_NV, C_QC, C_KV = (
    0, 8192, 12288, 13312, 14336, 15360, 16384, 17408, 18432, 19456, 19968)
NP_FULL = 20480
KV_BASE = C_NK


def _cparams(*sem):
    return pltpu.CompilerParams(dimension_semantics=sem, vmem_limit_bytes=VMEM_LIMIT_BYTES)


def _rms(x, g):
    return x * lax.rsqrt(jnp.mean(x * x, axis=-1, keepdims=True) + EPS) * g


def _silu(x):
    return x * jax.nn.sigmoid(x)


def _dot(a, b):
    return jnp.dot(a, b, preferred_element_type=F32)


def _dot_nt(a, b):
    return lax.dot_general(a, b, (((1,), (1,)), ((), ())), preferred_element_type=F32)


def _ada_kernel(c_ref, w_ref, b_ref, o_ref):
    a = _silu(c_ref[...]).astype(BF16)
    o_ref[0] = _dot(a, w_ref[0].astype(BF16)) + b_ref[0]


def _ada(cc, w_ada, b_ada):
    tn = 1024
    return pl.pallas_call(
        _ada_kernel,
        out_shape=jax.ShapeDtypeStruct((DEPTH, 8, 3 * D_MODEL), F32),
        grid=(DEPTH, 3 * D_MODEL // tn),
        in_specs=[pl.BlockSpec((8, D_MODEL), lambda l, j: (0, 0)),
                  pl.BlockSpec((1, D_MODEL, tn), lambda l, j: (l, 0, j)),
                  pl.BlockSpec((1, 1, tn), lambda l, j: (l, 0, j))],
        out_specs=pl.BlockSpec((1, 8, tn), lambda l, j: (l, 0, j)),
        compiler_params=_cparams("arbitrary", "arbitrary"),
        name="ada",
    )(cc, w_ada, b_ada)


def _prenorm_kernel(x_ref, sh_ref, sc_ref, g_ref, u_ref):
    y = _rms(x_ref[...], g_ref[...])
    u_ref[...] = (y * (1.0 + sc_ref[0]) + sh_ref[0]).astype(BF16)


def _prenorm(x2, mod3, row_of_tile, g_pre, tm):
    t = x2.shape[0]
    return pl.pallas_call(
        _prenorm_kernel,
        out_shape=jax.ShapeDtypeStruct((t, D_MODEL), BF16),
        grid=(t // tm,),
        in_specs=[pl.BlockSpec((tm, D_MODEL), lambda i: (i, 0)),
                  pl.BlockSpec((1, 1, D_MODEL), lambda i: (row_of_tile(i), 0, 0)),
                  pl.BlockSpec((1, 1, D_MODEL), lambda i: (row_of_tile(i), 0, 1)),
                  pl.BlockSpec((1, D_MODEL), lambda i: (0, 0))],
        out_specs=pl.BlockSpec((tm, D_MODEL), lambda i: (i, 0)),
        compiler_params=_cparams("arbitrary"),
        name="prenorm",
    )(x2, mod3, mod3, g_pre)


TN = 1024
W_ROWS = 256
N_ORIG = 20288

_TILE_SRC = ([12096 + TN * t for t in range(8)] + [8000 + TN * t for t in range(4)]
             + [3904 + TN * t for t in range(3)] + [6976, 2880, 320, 1344, -1])
SRC_QC = 2368
KV_ROWS = KV_LORA + MLA_ROPE


def _inproj_kernel(src_ref, u_ref, wt_hbm, o_ref, stage, wbuf, sems, *, layer, n_tiles):
    j = pl.program_id(0)

    def main_copy(jj):
        start = pl.multiple_of(src_ref[jj], 8)
        return pltpu.make_async_copy(wt_hbm.at[layer, pl.ds(start, TN), :], stage, sems.at[0])

    def tail_copies():
        return (pltpu.make_async_copy(wt_hbm.at[layer, pl.ds(SRC_QC, Q_LORA), :],
                                      stage.at[pl.ds(0, Q_LORA), :], sems.at[0]),
                pltpu.make_async_copy(wt_hbm.at[layer, pl.ds(0, KV_ROWS), :],
                                      stage.at[pl.ds(Q_LORA, KV_ROWS), :], sems.at[1]))

    def start(jj):
        @pl.when(src_ref[jj] >= 0)
        def _():
            main_copy(jj).start()

        @pl.when(src_ref[jj] < 0)
        def _():
            for c in tail_copies():
                c.start()

    def convert(n_rows):
        def body(r, carry):
            r0 = pl.multiple_of(r * W_ROWS, W_ROWS)
            wbuf[pl.ds(r0, W_ROWS), :] = stage[pl.ds(r0, W_ROWS), :].astype(BF16)
            return carry

        lax.fori_loop(0, n_rows // W_ROWS, body, 0)

    @pl.when(pl.program_id(1) == 0)
    def _():
        @pl.when(j == 0)
        def _():
            start(0)

        @pl.when(src_ref[j] >= 0)
        def _():
            main_copy(j).wait()
            convert(TN)

        @pl.when(src_ref[j] < 0)
        def _():
            for c in tail_copies():
                c.wait()
            convert(768)
            kr0 = Q_LORA + KV_LORA
            wbuf[768:832, :] = stage[768:832, :].astype(BF16)
            wbuf[832:848, :] = (-stage[kr0 + 16:kr0 + 32, :]).astype(BF16)
            wbuf[848:864, :] = stage[kr0:kr0 + 16, :].astype(BF16)
            wbuf[864:880, :] = (-stage[kr0 + 48:kr0 + 64, :]).astype(BF16)
            wbuf[880:896, :] = stage[kr0 + 32:kr0 + 48, :].astype(BF16)
            wbuf[896:TN, :] = jnp.zeros((TN - 896, D_MODEL), BF16)

        @pl.when(j + 1 < n_tiles)
        def _():
            start(j + 1)

    o_ref[...] = _dot_nt(u_ref[...], wbuf[...]).astype(o_ref.dtype)


def _inproj(u, wt_in, layer, tiles, tm):
    t = u.shape[0]
    src = np.array([_TILE_SRC[j] for j in tiles], np.int32)
    assert all(s % 8 == 0 and s + TN <= N_ORIG for s in src[src >= 0])
    grid_spec = pltpu.PrefetchScalarGridSpec(
        num_scalar_prefetch=1,
        grid=(len(tiles), t // tm),
        in_specs=[pl.BlockSpec((tm, D_MODEL), lambda j, i, s: (i, 0)),
                  pl.BlockSpec(memory_space=pl.ANY)],
        out_specs=pl.BlockSpec((tm, TN), lambda j, i, s: (i, j)),
        scratch_shapes=[pltpu.VMEM((TN, D_MODEL), F32), pltpu.VMEM((TN, D_MODEL), BF16),
                        pltpu.SemaphoreType.DMA((2,))])
    return pl.pallas_call(
        functools.partial(_inproj_kernel, layer=layer, n_tiles=len(tiles)),
        out_shape=jax.ShapeDtypeStruct((t, len(tiles) * TN), BF16),
        grid_spec=grid_spec,
        compiler_params=_cparams("arbitrary", "arbitrary"),
        name="inproj",
    )(jnp.asarray(src), u, wt_in)


def _qup_kernel(h_ref, g_ref, w_ref, cs_ref, o_ref):
    hn = _rms(h_ref[...].astype(F32), g_ref[...]).astype(BF16)
    qq = _dot(hn, w_ref[...]) * (MLA_SCALE * LOG2E)
    cs = cs_ref[...]
    for h in range(MLA_HEADS):
        o_ref[:, h * MLA_QK:h * MLA_QK + MLA_NOPE] = qq[:, h * 128:(h + 1) * 128].astype(BF16)
        t = qq[:, MLA_W + h * 128:MLA_W + (h + 1) * 128] * cs
        o_ref[:, h * MLA_QK + MLA_NOPE:(h + 1) * MLA_QK] = (t + pltpu.roll(t, 64, 1)).astype(BF16)


def _qup(p, col, g_q, w_q, cs, n_seq, tm):
    t = p.shape[0]
    spt = n_seq // tm
    return pl.pallas_call(
        _qup_kernel,
        out_shape=jax.ShapeDtypeStruct((t, MLA_HEADS * MLA_QK), BF16),
        grid=(t // tm,),
        in_specs=[pl.BlockSpec((tm, Q_LORA), lambda i: (i, col // Q_LORA)),
                  pl.BlockSpec((1, Q_LORA), lambda i: (0, 0)),
                  pl.BlockSpec((Q_LORA, 2 * MLA_W), lambda i: (0, 0)),
                  pl.BlockSpec((tm, 128), lambda i: (i % spt, 0))],
        out_specs=pl.BlockSpec((tm, MLA_HEADS * MLA_QK), lambda i: (i, 0)),
        compiler_params=_cparams("arbitrary"),
        name="mla_qup",
    )(p, g_q, w_q, cs)


def _kvup_kernel(p_ref, g_ref, wk_ref, wvt_ref, cs_ref, k_ref, vt_ref):
    p = p_ref[...]
    cn = _rms(p[:, :KV_LORA].astype(F32), g_ref[...]).astype(BF16)
    kn = _dot(cn, wk_ref[...])
    t = p[:, KV_LORA:KV_LORA + 128].astype(F32) * cs_ref[...]
    lane = lax.broadcasted_iota(jnp.int32, t.shape, 1)
    kr = jnp.where(lane < MLA_ROPE, t + pltpu.roll(t, 64, 1), 0.0).astype(BF16)
    for h in range(MLA_HEADS):
        k_ref[:, h * MLA_QK:h * MLA_QK + MLA_NOPE] = kn[:, h * 128:(h + 1) * 128].astype(BF16)
        k_ref[:, h * MLA_QK + MLA_NOPE:(h + 1) * MLA_QK] = kr
    vt_ref[0] = _dot_nt(wvt_ref[...], cn).astype(BF16)


def _kvup(p, col, g_kv, w_k, w_vt, cs, n_seq, tm):
    t = p.shape[0]
    spt = n_seq // tm
    return pl.pallas_call(
        _kvup_kernel,
        out_shape=(jax.ShapeDtypeStruct((t, MLA_HEADS * MLA_QK), BF16),
                   jax.ShapeDtypeStruct((t // n_seq, MLA_W, n_seq), BF16)),
        grid=(t // tm,),
        in_specs=[pl.BlockSpec((tm, KV_SEG), lambda i: (i, col // KV_SEG)),
                  pl.BlockSpec((1, KV_LORA), lambda i: (0, 0)),
                  pl.BlockSpec((KV_LORA, MLA_W), lambda i: (0, 0)),
                  pl.BlockSpec((MLA_W, KV_LORA), lambda i: (0, 0)),
                  pl.BlockSpec((tm, 128), lambda i: (i % spt, 0))],
        out_specs=(pl.BlockSpec((tm, MLA_HEADS * MLA_QK), lambda i: (i, 0)),
                   pl.BlockSpec((1, MLA_W, tm), lambda i: (i // spt, 0, i % spt))),
        compiler_params=_cparams("arbitrary"),
        name="mla_kvup",
    )(p, g_kv, w_k, w_vt, cs)


MLA_ROWS = 512
MLA_KEYS = 256


def _mla_attn_kernel(q_ref, kc_ref, vtc_ref, *rest, has_lat):
    if has_lat:
        kl_ref, vtl_ref, o_ref = rest
    else:
        (o_ref,) = rest
    for r0 in range(0, q_ref.shape[0], MLA_ROWS):
        rows = slice(r0, r0 + MLA_ROWS)
        q = q_ref[rows, :]
        st_c = _dot_nt(kc_ref[...], q)
        m = jnp.max(st_c, axis=0, keepdims=True)
        st_l = []
        if has_lat:
            for k0 in range(0, kl_ref.shape[0], MLA_KEYS):
                st = _dot_nt(kl_ref[k0:k0 + MLA_KEYS, :], q)
                m = jnp.maximum(m, jnp.max(st, axis=0, keepdims=True))
                st_l.append(st)
        pt_c = jnp.exp2(st_c - m)
        den = jnp.sum(pt_c, axis=0, keepdims=True)
        ot = _dot(vtc_ref[0], pt_c.astype(BF16))
        for i, st in enumerate(st_l):
            pt = jnp.exp2(st - m)
            den = den + jnp.sum(pt, axis=0, keepdims=True)
            ot = ot + _dot(vtl_ref[0, :, i * MLA_KEYS:(i + 1) * MLA_KEYS], pt.astype(BF16))
        o_ref[rows, :] = (ot / den).T.astype(o_ref.dtype)


def _mla_attn(q, kc, vtc, kl, vtl, n_batch, n_q, n_ctx, tq):
    has_lat = kl is not None
    qt = n_q // tq
    in_specs = [pl.BlockSpec((tq, MLA_QK), lambda b, h, i: (b * qt + i, h)),
                pl.BlockSpec((n_ctx, MLA_QK), lambda b, h, i: (b, h)),
                pl.BlockSpec((1, MLA_V, n_ctx), lambda b, h, i: (b, h, 0))]
    args = [q, kc, vtc]
    if has_lat:
        in_specs += [pl.BlockSpec((n_q, MLA_QK), lambda b, h, i: (b, h)),
                     pl.BlockSpec((1, MLA_V, n_q), lambda b, h, i: (b, h, 0))]
        args += [kl, vtl]
    return pl.pallas_call(
        functools.partial(_mla_attn_kernel, has_lat=has_lat),
        out_shape=jax.ShapeDtypeStruct((n_batch * n_q, MLA_W), BF16),
        grid=(n_batch, MLA_HEADS, qt),
        in_specs=in_specs,
        out_specs=pl.BlockSpec((tq, MLA_V), lambda b, h, i: (b * qt + i, h)),
        compiler_params=_cparams("arbitrary", "arbitrary", "arbitrary"),
        name="mla_attn_lat" if has_lat else "mla_attn_ctx",
    )(*args)


def _pair_rows(q2):
    lane = lax.broadcasted_iota(jnp.int32, q2.shape, 1)
    zero = jnp.zeros_like(q2)
    return jnp.concatenate([jnp.where(lane < NA_HD, q2, zero), jnp.where(lane >= NA_HD, q2, zero)], axis=0)


def _unpair_rows(o, n):
    lane = lax.broadcasted_iota(jnp.int32, (n, 128), 1)
    return jnp.where(lane < NA_HD, o[:n], o[n:])


NA_G = 4
NA_SLAB = NA_G + NA_KH
NA_NEG_ENTRY = 2 * NA_KH


def _na_lat_kernel(q_ref, k_ref, v_ref, kc_ref, vc_ref, b_ref, o_ref):
    g = pl.program_id(1)
    rows = k_ref.shape[0] // GRID_W
    nq = NA_G * GRID_W
    us = jnp.clip(NA_G * g - NA_KH // 2, 0, rows - NA_SLAB)
    start = pl.multiple_of(us * GRID_W, GRID_W)

    entry, ok_l, ok_r = {}, {}, {}
    for dr in range(NA_G):
        r = NA_G * g + dr
        rs = jnp.clip(r - NA_KH // 2, 0, rows - NA_KH)
        for p in range(NA_SLAB // 2):
            kr = us + 2 * p
            l_ok = jnp.logical_and(kr >= rs, kr < rs + NA_KH)
            r_ok = jnp.logical_and(kr + 1 >= rs, kr + 1 < rs + NA_KH)
            ro = kr - r + (NA_KH - 1)
            entry[dr, p] = jnp.where(jnp.logical_or(l_ok, r_ok), jnp.clip(ro + 1, 0, NA_NEG_ENTRY - 1),
                                     NA_NEG_ENTRY)
            ok_l[dr, p] = l_ok.astype(jnp.int32)
            ok_r[dr, p] = r_ok.astype(jnp.int32)
    lane = lax.broadcasted_iota(jnp.int32, (GRID_W, 128), 1)

    def bias_tile(hp, sub, dr, p):
        t = b_ref[hp, entry[dr, p], sub * GRID_W:(sub + 1) * GRID_W, :]
        if dr % 2 == 1 and p in ((dr - 1) // 2, (dr + NA_KH - 1) // 2):
            t = jnp.where(jnp.where(lane < GRID_W, ok_l[dr, p], ok_r[dr, p]) > 0, t, NEG)
        return t

    for hp in range(NA_HEADS // 2):
        cols = slice(hp * 128, (hp + 1) * 128)
        qm = _pair_rows(q_ref[:, cols] * NA_SCALE)
        k2 = k_ref[pl.ds(start, NA_SLAB * GRID_W), cols]
        v2 = v_ref[pl.ds(start, NA_SLAB * GRID_W), cols]
        bias = jnp.concatenate(
            [jnp.concatenate([bias_tile(hp, sub, dr, p) for p in range(NA_SLAB // 2)], axis=1)
             for sub in range(2) for dr in range(NA_G)], axis=0)
        s_nb = _dot_nt(qm, k2) + bias
        s_cx = _dot_nt(qm, kc_ref[:, cols])
        m = jnp.maximum(jnp.max(s_nb, axis=-1, keepdims=True), jnp.max(s_cx, axis=-1, keepdims=True))
        p_nb = jnp.exp(s_nb - m)
        p_cx = jnp.exp(s_cx - m)
        den = jnp.sum(p_nb, axis=-1, keepdims=True) + jnp.sum(p_cx, axis=-1, keepdims=True)
        o = (_dot(p_nb.astype(BF16), v2) + _dot(p_cx.astype(BF16), vc_ref[:, cols])) / den
        o_ref[:, cols] = _unpair_rows(o, nq).astype(o_ref.dtype)


def _na_lat(p, pc, cq, ck, cv, cck, ccv, bias, n_batch, n_seq, n_ctx):
    rows = n_seq // GRID_W
    assert rows % NA_G == 0 and rows >= NA_SLAB
    groups = rows // NA_G
    nq = NA_G * GRID_W
    return pl.pallas_call(
        _na_lat_kernel,
        out_shape=jax.ShapeDtypeStruct((n_batch * n_seq, NA_W), BF16),
        grid=(n_batch, groups),
        in_specs=[pl.BlockSpec((nq, NA_W), lambda b, g: (b * groups + g, cq // NA_W)),
                  pl.BlockSpec((n_seq, NA_W), lambda b, g: (b, ck // NA_W)),
                  pl.BlockSpec((n_seq, NA_W), lambda b, g: (b, cv // NA_W)),
                  pl.BlockSpec((n_ctx, NA_W), lambda b, g: (b, cck // NA_W)),
                  pl.BlockSpec((n_ctx, NA_W), lambda b, g: (b, ccv // NA_W)),
                  pl.BlockSpec((NA_HEADS // 2, NA_NEG_ENTRY + 1, 2 * GRID_W, 2 * GRID_W),
                               lambda b, g: (0, 0, 0, 0), pipeline_mode=pl.Buffered(1))],
        out_specs=pl.BlockSpec((nq, NA_W), lambda b, g: (b * groups + g, 0)),
        compiler_params=_cparams("arbitrary", "arbitrary"),
        name="na_lat",
    )(p, p, p, pc, pc, bias)


def _na_ctx_kernel(q_ref, kc_ref, vc_ref, o_ref):
    n = q_ref.shape[0]
    for hp in range(NA_HEADS // 2):
        cols = slice(hp * 128, (hp + 1) * 128)
        qm = _pair_rows(q_ref[:, cols] * NA_SCALE)
        s = _dot_nt(qm, kc_ref[:, cols])
        p = jnp.exp(s - jnp.max(s, axis=-1, keepdims=True))
        o = _dot(p.astype(BF16), vc_ref[:, cols]) / jnp.sum(p, axis=-1, keepdims=True)
        o_ref[:, cols] = _unpair_rows(o, n).astype(o_ref.dtype)


def _na_ctx(pc, cq, ck, cv, n_batch, n_ctx):
    return pl.pallas_call(
        _na_ctx_kernel,
        out_shape=jax.ShapeDtypeStruct((n_batch * n_ctx, NA_W), BF16),
        grid=(n_batch,),
        in_specs=[pl.BlockSpec((n_ctx, NA_W), lambda b: (b, cq // NA_W)),
                  pl.BlockSpec((n_ctx, NA_W), lambda b: (b, ck // NA_W)),
                  pl.BlockSpec((n_ctx, NA_W), lambda b: (b, cv // NA_W))],
        out_specs=pl.BlockSpec((n_ctx, NA_W), lambda b: (b, 0)),
        compiler_params=_cparams("arbitrary"),
        name="na_ctx",
    )(pc, pc, pc)


def _na_bias_table(rpb):
    c = np.arange(GRID_W)[:, None]
    kc = np.arange(GRID_W)[None, :]
    cs = np.clip(c - NA_KW // 2, 0, GRID_W - NA_KW)
    valid = (kc >= cs) & (kc < cs + NA_KW)
    co = kc - c + (NA_KW - 1)
    onehot = (valid[None] & (co[None] == np.arange(2 * NA_KW - 1)[:, None, None])).astype(np.float32)
    t = jnp.einsum('hrj,jck->hrck', rpb.astype(F32), onehot, precision=lax.Precision.HIGHEST)
    t = t + np.where(valid, 0.0, NEG).astype(np.float32)
    t = t.reshape(NA_HEADS // 2, 2, 2 * NA_KH - 1, GRID_W, GRID_W).transpose(0, 2, 1, 3, 4)
    t = t.reshape(NA_HEADS // 2, 2 * NA_KH - 1, 2 * GRID_W, GRID_W)
    neg = jnp.full((NA_HEADS // 2, 1, 2 * GRID_W, GRID_W), NEG, F32)
    t = jnp.concatenate([neg, t, neg, neg], axis=1)
    return jnp.concatenate([t[:, :-1], t[:, 1:]], axis=-1)


def _conv_kernel(cb_ref, cc_ref, cx_ref, w_ref, o_ref):
    z = cc_ref[...].astype(F32) * cx_ref[...].astype(F32)
    n = z.shape[0]
    row = lax.broadcasted_iota(jnp.int32, z.shape, 0)
    zp = jnp.where(row == 0, 0.0, pltpu.roll(z, 1, 0))
    zn = jnp.where(row == n - 1, 0.0, pltpu.roll(z, n - 1, 0))
    w = w_ref[...]
    conv = zp * w[0:1] + z * w[1:2] + zn * w[2:3]
    o_ref[...] = (cb_ref[...].astype(F32) * conv).astype(o_ref.dtype)


def _conv(p, conv_w, n_batch, n_seq):
    tc = 256
    nb = CONV_W // tc
    return pl.pallas_call(
        _conv_kernel,
        out_shape=jax.ShapeDtypeStruct((n_batch * n_seq, CONV_W), BF16),
        grid=(n_batch, nb),
        in_specs=[pl.BlockSpec((n_seq, tc), lambda b, j: (b, C_CB // tc + j)),
                  pl.BlockSpec((n_seq, tc), lambda b, j: (b, C_CC // tc + j)),
                  pl.BlockSpec((n_seq, tc), lambda b, j: (b, C_CX // tc + j)),
                  pl.BlockSpec((CONV_K, tc), lambda b, j: (0, j))],
        out_specs=pl.BlockSpec((n_seq, tc), lambda b, j: (b, j)),
        compiler_params=_cparams("arbitrary", "arbitrary"),
        name="conv",
    )(p, p, p, conv_w)


def _fourier_kernel(x_ref, cg_ref, sg_ref, ch_ref, sh_ref, j_ref, o_ref, *, norm):
    x = x_ref[...]
    n = x.shape[0]
    h = n // 2
    rb = j_ref.shape[0]
    jm = j_ref[...]

    def rev(t):
        m = t.shape[0]
        return jnp.concatenate([_dot(jm, t[m - (i + 1) * rb:m - i * rb, :]) for i in range(m // rb)], axis=0)

    def mirror(t, first_row):
        row = lax.broadcasted_iota(jnp.int32, t.shape, 0)
        return jnp.where(row == 0, first_row, pltpu.roll(rev(t), 1, 0))

    a = _dot(x, cg_ref[...])
    b = _dot(x, sg_ref[...])
    ta = mirror(a.astype(BF16), 0.0)
    tb = mirror(b.astype(BF16), 0.0)
    ae = (a + ta)[:h].astype(BF16)
    bo = (b - tb)[:h].astype(BF16)
    rowh = lax.broadcasted_iota(jnp.int32, (h, x.shape[1]), 0)
    sign_h = jnp.where(rowh % 2 == 0, 1.0, -1.0)
    e = _dot(ch_ref[...], ae) + sign_h * a[h:h + 1, :]
    o = _dot(sh_ref[...], bo)
    o_ref[:h, :] = ((e - o) * norm).astype(o_ref.dtype)
    rown = lax.broadcasted_iota(jnp.int32, a.shape, 0)
    mid = jnp.sum(jnp.where(rown % 2 == 0, a, -a), axis=0, keepdims=True)
    g = ((e + o) * norm).astype(BF16)
    o_ref[h:, :] = mirror(g, mid * norm).astype(o_ref.dtype)


@functools.lru_cache(maxsize=None)
def _dft_tables(n):
    j = np.arange(n, dtype=np.int64)
    ang = ((j[:, None] * j[None, :]) % n).astype(np.float64) * (2.0 * np.pi / n)
    return np.cos(ang).astype(np.float32), np.sin(ang).astype(np.float32)


def _fourier(p, n_batch, n_seq):
    h = n_seq // 2
    rb = min(256, h)
    assert h % rb == 0
    cg, sg = (jnp.asarray(t).astype(BF16) for t in _dft_tables(FNET_GW))
    ch, sh = (jnp.asarray(np.ascontiguousarray(t[:h, :h])).astype(BF16) for t in _dft_tables(n_seq))
    flip = jnp.asarray(np.eye(rb, dtype=np.float32)[::-1].copy()).astype(BF16)
    const = lambda b, g: (0, 0)
    return pl.pallas_call(
        functools.partial(_fourier_kernel, norm=float((n_seq * FNET_GW) ** -0.5)),
        out_shape=jax.ShapeDtypeStruct((n_batch * n_seq, FNET_W), BF16),
        grid=(n_batch, FNET_GROUPS),
        in_specs=[pl.BlockSpec((n_seq, FNET_GW), lambda b, g: (b, C_FV // FNET_GW + g)),
                  pl.BlockSpec((FNET_GW, FNET_GW), const),
                  pl.BlockSpec((FNET_GW, FNET_GW), const),
                  pl.BlockSpec((h, h), const, pipeline_mode=pl.Buffered(1)),
                  pl.BlockSpec((h, h), const, pipeline_mode=pl.Buffered(1)),
                  pl.BlockSpec((rb, rb), const)],
        out_specs=pl.BlockSpec((n_seq, FNET_GW), lambda b, g: (b, g)),
        compiler_params=_cparams("arbitrary", "arbitrary"),
        name="fourier",
    )(p, cg, sg, ch, sh, flip)


def _combine_kernel(o0_ref, o1_ref, o2_ref, o3_ref, g_ref, mg_ref, w_ref, m_ref):
    acc = None
    for k, o_ref in enumerate((o0_ref, o1_ref, o2_ref, o3_ref)):
        g = g_ref[:, k * 1024:(k + 1) * 1024].astype(F32)
        t = (o_ref[...].astype(F32) * _silu(g)).astype(BF16)
        gate = jax.nn.sigmoid(mg_ref[:, k * D_MODEL:(k + 1) * D_MODEL].astype(F32))
        r = _dot(t, w_ref[k]) * gate
        acc = r if acc is None else acc + r
    m_ref[...] = acc.astype(m_ref.dtype)


def _combine(p, outs, w_p, tm):
    t = p.shape[0]
    o_spec = pl.BlockSpec((tm, 1024), lambda i: (i, 0))
    return pl.pallas_call(
        _combine_kernel,
        out_shape=jax.ShapeDtypeStruct((t, D_MODEL), BF16),
        grid=(t // tm,),
        in_specs=[o_spec, o_spec, o_spec, o_spec,
                  pl.BlockSpec((tm, N_BRANCH * 1024), lambda i: (i, C_G // (N_BRANCH * 1024))),
                  pl.BlockSpec((tm, N_BRANCH * D_MODEL), lambda i: (i, C_MG // (N_BRANCH * D_MODEL))),
                  pl.BlockSpec((N_BRANCH, 1024, D_MODEL), lambda i: (0, 0, 0), pipeline_mode=pl.Buffered(1))],
        out_specs=pl.BlockSpec((tm, D_MODEL), lambda i: (i, 0)),
        compiler_params=_cparams("arbitrary"),
        name="combine",
    )(*outs, p, p, w_p)


OUT_ROWS = 256


def _outproj_kernel(m_ref, w_ref, x_ref, gt_ref, g_ref, *rest, with_next):
    if with_next:
        sh_ref, sc_ref, gn_ref, o_ref, u_ref = rest
    else:
        (o_ref,) = rest
    for r0 in range(0, m_ref.shape[0], OUT_ROWS):
        rows = slice(r0, r0 + OUT_ROWS)
        y = _dot(m_ref[rows, :], w_ref[...])
        x_new = x_ref[rows, :] + gt_ref[0] * _rms(y, g_ref[...])
        if with_next:
            u_ref[rows, :] = (_rms(x_new, gn_ref[...]) * (1.0 + sc_ref[0]) + sh_ref[0]).astype(BF16)
        o_ref[rows, :] = x_new


def _outproj(m, w_out, x2, mod3, row_of_tile, g_post, tm, nxt=None):
    t = m.shape[0]
    tile = pl.BlockSpec((tm, D_MODEL), lambda i: (i, 0))
    vec = pl.BlockSpec((1, D_MODEL), lambda i: (0, 0))

    def mod_spec(part):
        return pl.BlockSpec((1, 1, D_MODEL), lambda i: (row_of_tile(i), 0, part))

    in_specs = [tile, pl.BlockSpec((D_MODEL, D_MODEL), lambda i: (0, 0), pipeline_mode=pl.Buffered(1)),
                tile, mod_spec(2), vec]
    args = [m, w_out, x2, mod3, g_post]
    out_shape = jax.ShapeDtypeStruct((t, D_MODEL), F32)
    out_specs = tile
    if nxt is not None:
        in_specs += [mod_spec(0), mod_spec(1), vec]
        args += [nxt[0], nxt[0], nxt[1]]
        out_shape = (out_shape, jax.ShapeDtypeStruct((t, D_MODEL), BF16))
        out_specs = (tile, tile)
    return pl.pallas_call(
        functools.partial(_outproj_kernel, with_next=nxt is not None),
        out_shape=out_shape,
        grid=(t // tm,),
        in_specs=in_specs,
        out_specs=out_specs,
        compiler_params=_cparams("arbitrary"),
        name="outproj",
    )(*args)


def _rope_swap(w):
    return jnp.concatenate([-w[..., 16:32], w[..., 0:16], -w[..., 48:64], w[..., 32:48]], axis=-1)


def _prep_w_uq(w):
    w3 = w.reshape(Q_LORA, MLA_HEADS, MLA_NOPE + MLA_ROPE)
    nope = w3[:, :, :MLA_NOPE].reshape(Q_LORA, MLA_W)
    r = w3[:, :, MLA_NOPE:]
    rr = jnp.concatenate([r, _rope_swap(r)], axis=-1).reshape(Q_LORA, MLA_HEADS * 128)
    return jnp.concatenate([nope, rr], axis=1).astype(BF16)


def _prep_w_ukv(w):
    w3 = w.reshape(KV_LORA, MLA_HEADS, MLA_NOPE + MLA_V)
    return (w3[:, :, :MLA_NOPE].reshape(KV_LORA, MLA_W).astype(BF16),
            w3[:, :, MLA_NOPE:].reshape(KV_LORA, MLA_W).T.astype(BF16))


def _rope_table(n):
    nf = MLA_ROPE // 4
    t = jnp.arange(n, dtype=jnp.int32)
    pos = jnp.stack([t // GRID_W, t % GRID_W], axis=-1).astype(F32)
    inv = ROPE_THETA ** (-jnp.arange(nf, dtype=F32) / nf)
    ang = pos[:, :, None] * inv
    cos = jnp.repeat(jnp.cos(ang), 2, axis=1).reshape(n, MLA_ROPE)
    sin = jnp.repeat(jnp.sin(ang), 2, axis=1).reshape(n, MLA_ROPE)
    return jnp.concatenate([cos, sin], axis=-1).astype(F32)


def _identity_rope_table(n):
    return jnp.concatenate([jnp.ones((n, MLA_ROPE), F32), jnp.zeros((n, MLA_ROPE), F32)], axis=-1)


def kernel(x, c, ctx, c_ctx, g_pre, g_post, w_ada, b_ada, w_in, g_q, g_kv, w_uq, w_ukv, conv_w, na_rpb,
           w_p_conv, w_p_mla, w_p_na, w_p_fnet, w_out):
    n_batch, n_seq, _ = x.shape
    n_ctx = ctx.shape[1]
    assert n_batch + 1 <= 8 and n_seq % 1024 == 0 and n_ctx % 256 == 0
    xl = x.reshape(n_batch * n_seq, D_MODEL)
    xc = ctx.reshape(n_batch * n_ctx, D_MODEL)

    cc = jnp.concatenate([c, c_ctx[None], jnp.zeros((8 - n_batch - 1, D_MODEL), c.dtype)], axis=0)
    mod = _ada(cc, w_ada, b_ada.reshape(DEPTH, 1, 3 * D_MODEL))

    wt_in = jnp.swapaxes(w_in, 1, 2)
    rope_l = _rope_table(n_seq)
    rope_c = _identity_rope_table(n_ctx)
    tm_l = min(2048, n_batch * n_seq)
    tm_c = n_batch * n_ctx
    ctx_row = lambda i: n_batch

    def lat_row_t(tm):
        return lambda i: i * tm // n_seq

    for l in range(DEPTH):
        ctx_out = l < DEPTH - 1
        mod3 = mod[l].reshape(8, 1, 3 * D_MODEL)
        w_q = _prep_w_uq(w_uq[l])
        w_k, w_vt = _prep_w_ukv(w_ukv[l])
        gq = g_q[l].reshape(1, Q_LORA)
        gkv = g_kv[l].reshape(1, KV_LORA)
        gpre = g_pre[l].reshape(1, D_MODEL)
        gpost = g_post[l].reshape(1, D_MODEL)
        w_p = jnp.stack([w_p_conv[l], w_p_mla[l], w_p_na[l], w_p_fnet[l]]).astype(BF16)
        w_o = w_out[l].astype(BF16)
        bias = _na_bias_table(na_rpb[l])

        all_tiles = tuple(range(NP_FULL // TN))
        if l == 0:
            ul = _prenorm(xl, mod3, lat_row_t(512), gpre, 512)
            uc = _prenorm(xc, mod3, ctx_row, gpre, min(512, tm_c))
        nxt = None
        if l + 1 < DEPTH:
            nxt = (mod[l + 1].reshape(8, 1, 3 * D_MODEL), g_pre[l + 1].reshape(1, D_MODEL))
        pl_ = _inproj(ul, wt_in, l, all_tiles, tm_l)
        if ctx_out:
            pc = _inproj(uc, wt_in, l, all_tiles, tm_c)
            base = 0
        else:
            pc = _inproj(uc, wt_in, l, all_tiles[KV_BASE // TN:], tm_c)
            base = KV_BASE

        kc_m, vc_m = _kvup(pc, C_KV - base, gkv, w_k, w_vt, rope_c, n_ctx, n_ctx)
        kl_m, vl_m = _kvup(pl_, C_KV, gkv, w_k, w_vt, rope_l, n_seq, 512)
        ql_m = _qup(pl_, C_QC, gq, w_q, rope_l, n_seq, 512)
        mla_l = _mla_attn(ql_m, kc_m, vc_m, kl_m, vl_m, n_batch, n_seq, n_ctx, min(2048, n_seq))
        na_l = _na_lat(pl_, pc, C_NQ, C_NK, C_NV, C_NK - base, C_NV - base, bias, n_batch, n_seq, n_ctx)
        conv_l = _conv(pl_, conv_w[l], n_batch, n_seq)
        fn_l = _fourier(pl_, n_batch, n_seq)
        m_l = _combine(pl_, (conv_l, mla_l, na_l, fn_l), w_p, 256)
        xl_new = _outproj(m_l, w_o, xl, mod3, lat_row_t(512), gpost, 512, nxt)
        if nxt is not None:
            xl_new, ul = xl_new

        if ctx_out:
            qc_m = _qup(pc, C_QC, gq, w_q, rope_c, n_ctx, n_ctx)
            mla_c = _mla_attn(qc_m, kc_m, vc_m, None, None, n_batch, n_ctx, n_ctx, n_ctx)
            na_c = _na_ctx(pc, C_NQ, C_NK, C_NV, n_batch, n_ctx)
            conv_c = _conv(pc, conv_w[l], n_batch, n_ctx)
            fn_c = _fourier(pc, n_batch, n_ctx)
            m_c = _combine(pc, (conv_c, mla_c, na_c, fn_c), w_p, 256)
            xc, uc = _outproj(m_c, w_o, xc, mod3, ctx_row, gpost, min(512, tm_c), nxt)
        xl = xl_new

    return xl.reshape(n_batch, n_seq, D_MODEL)
```

```python
import functools

import jax
import jax.numpy as jnp
import numpy as np
from jax import lax
from jax.experimental import pallas as pl
from jax.experimental.pallas import tpu as pltpu

F32 = jnp.float32
BF16 = jnp.bfloat16

D_MODEL = 2048
DEPTH = 2
GRID_W = 64
CONV_W = 1024
CONV_K = 3
MLA_HEADS = 8
MLA_NOPE = 128
MLA_ROPE = 64
MLA_V = 128
Q_LORA = 512
KV_LORA = 256
MLA_W = MLA_HEADS * MLA_V
MLA_SCALE = (MLA_NOPE + MLA_ROPE) ** -0.5
LOG2E = 1.4426950408889634
NA_HEADS = 16
NA_HD = 64
NA_W = NA_HEADS * NA_HD
NA_KH = 8
NA_KW = 16
NA_SCALE = NA_HD ** -0.5
FNET_GROUPS = 4
FNET_GW = 256
FNET_W = FNET_GROUPS * FNET_GW
N_BRANCH = 4
ROPE_THETA = 10000.0
EPS = 1e-6
NEG = -1e30

VMEM_LIMIT_BYTES = 56 * 1024 * 1024

MLA_QK = 256
KV_SEG = 512
C_MG, C_G, C_CB, C_CC, C_CX, C_FV, C_NQ, C_NK, C_NV, C_QC, C_KV = (
    0, 8192, 12288, 13312, 14336, 15360, 16384, 17408, 18432, 19456, 19968)
NP_FULL = 20480
KV_BASE = C_NK


def _cparams(*sem):
    return pltpu.CompilerParams(dimension_semantics=sem, vmem_limit_bytes=VMEM_LIMIT_BYTES)


def _rms(x, g):
    return x * lax.rsqrt(jnp.mean(x * x, axis=-1, keepdims=True) + EPS) * g


def _silu(x):
    return x * jax.nn.sigmoid(x)


def _dot(a, b):
    return jnp.dot(a, b, preferred_element_type=F32)


def _dot_nt(a, b):
    return lax.dot_general(a, b, (((1,), (1,)), ((), ())), preferred_element_type=F32)


def _ada_kernel(c_ref, w_ref, b_ref, o_ref):
    a = _silu(c_ref[...]).astype(BF16)
    o_ref[0] = _dot(a, w_ref[0].astype(BF16)) + b_ref[0]


def _ada(cc, w_ada, b_ada):
    tn = 1024
    return pl.pallas_call(
        _ada_kernel,
        out_shape=jax.ShapeDtypeStruct((DEPTH, 8, 3 * D_MODEL), F32),
        grid=(DEPTH, 3 * D_MODEL // tn),
        in_specs=[pl.BlockSpec((8, D_MODEL), lambda l, j: (0, 0)),
                  pl.BlockSpec((1, D_MODEL, tn), lambda l, j: (l, 0, j)),
                  pl.BlockSpec((1, 1, tn), lambda l, j: (l, 0, j))],
        out_specs=pl.BlockSpec((1, 8, tn), lambda l, j: (l, 0, j)),
        compiler_params=_cparams("arbitrary", "arbitrary"),
        name="ada",
    )(cc, w_ada, b_ada)


def _prenorm_kernel(x_ref, sh_ref, sc_ref, g_ref, u_ref):
    y = _rms(x_ref[...], g_ref[...])
    u_ref[...] = (y * (1.0 + sc_ref[0]) + sh_ref[0]).astype(BF16)


def _prenorm(x2, mod3, row_of_tile, g_pre, tm):
    t = x2.shape[0]
    return pl.pallas_call(
        _prenorm_kernel,
        out_shape=jax.ShapeDtypeStruct((t, D_MODEL), BF16),
        grid=(t // tm,),
        in_specs=[pl.BlockSpec((tm, D_MODEL), lambda i: (i, 0)),
                  pl.BlockSpec((1, 1, D_MODEL), lambda i: (row_of_tile(i), 0, 0)),
                  pl.BlockSpec((1, 1, D_MODEL), lambda i: (row_of_tile(i), 0, 1)),
                  pl.BlockSpec((1, D_MODEL), lambda i: (0, 0))],
        out_specs=pl.BlockSpec((tm, D_MODEL), lambda i: (i, 0)),
        compiler_params=_cparams("arbitrary"),
        name="prenorm",
    )(x2, mod3, mod3, g_pre)


TN = 1024
W_ROWS = 256
N_ORIG = 20288

_TILE_SRC = ([12096 + TN * t for t in range(8)] + [8000 + TN * t for t in range(4)]
             + [3904 + TN * t for t in range(3)] + [6976, 2880, 320, 1344, -1])
SRC_QC = 2368
KV_ROWS = KV_LORA + MLA_ROPE


def _inproj_kernel(src_ref, u_ref, wt_hbm, o_ref, stage, wbuf, sems, *, layer, n_tiles):
    j = pl.program_id(0)

    def main_copy(jj):
        start = pl.multiple_of(src_ref[jj], 8)
        return pltpu.make_async_copy(wt_hbm.at[layer, pl.ds(start, TN), :], stage, sems.at[0])

    def tail_copies():
        return (pltpu.make_async_copy(wt_hbm.at[layer, pl.ds(SRC_QC, Q_LORA), :],
                                      stage.at[pl.ds(0, Q_LORA), :], sems.at[0]),
                pltpu.make_async_copy(wt_hbm.at[layer, pl.ds(0, KV_ROWS), :],
                                      stage.at[pl.ds(Q_LORA, KV_ROWS), :], sems.at[1]))

    def start(jj):
        @pl.when(src_ref[jj] >= 0)
        def _():
            main_copy(jj).start()

        @pl.when(src_ref[jj] < 0)
        def _():
            for c in tail_copies():
                c.start()

    def convert(n_rows):
        def body(r, carry):
            r0 = pl.multiple_of(r * W_ROWS, W_ROWS)
            wbuf[pl.ds(r0, W_ROWS), :] = stage[pl.ds(r0, W_ROWS), :].astype(BF16)
            return carry

        lax.fori_loop(0, n_rows // W_ROWS, body, 0)

    @pl.when(pl.program_id(1) == 0)
    def _():
        @pl.when(j == 0)
        def _():
            start(0)

        @pl.when(src_ref[j] >= 0)
        def _():
            main_copy(j).wait()
            convert(TN)

        @pl.when(src_ref[j] < 0)
        def _():
            for c in tail_copies():
                c.wait()
            convert(768)
            kr0 = Q_LORA + KV_LORA
            wbuf[768:832, :] = stage[768:832, :].astype(BF16)
            wbuf[832:848, :] = (-stage[kr0 + 16:kr0 + 32, :]).astype(BF16)
            wbuf[848:864, :] = stage[kr0:kr0 + 16, :].astype(BF16)
            wbuf[864:880, :] = (-stage[kr0 + 48:kr0 + 64, :]).astype(BF16)
            wbuf[880:896, :] = stage[kr0 + 32:kr0 + 48, :].astype(BF16)
            wbuf[896:TN, :] = jnp.zeros((TN - 896, D_MODEL), BF16)

        @pl.when(j + 1 < n_tiles)
        def _():
            start(j + 1)

    o_ref[...] = _dot_nt(u_ref[...], wbuf[...]).astype(o_ref.dtype)


def _inproj(u, wt_in, layer, tiles, tm):
    t = u.shape[0]
    src = np.array([_TILE_SRC[j] for j in tiles], np.int32)
    assert all(s % 8 == 0 and s + TN <= N_ORIG for s in src[src >= 0])
    grid_spec = pltpu.PrefetchScalarGridSpec(
        num_scalar_prefetch=1,
        grid=(len(tiles), t // tm),
        in_specs=[pl.BlockSpec((tm, D_MODEL), lambda j, i, s: (i, 0)),
                  pl.BlockSpec(memory_space=pl.ANY)],
        out_specs=pl.BlockSpec((tm, TN), lambda j, i, s: (i, j)),
        scratch_shapes=[pltpu.VMEM((TN, D_MODEL), F32), pltpu.VMEM((TN, D_MODEL), BF16),
                        pltpu.SemaphoreType.DMA((2,))])
    return pl.pallas_call(
        functools.partial(_inproj_kernel, layer=layer, n_tiles=len(tiles)),
        out_shape=jax.ShapeDtypeStruct((t, len(tiles) * TN), BF16),
        grid_spec=grid_spec,
        compiler_params=_cparams("arbitrary", "arbitrary"),
        name="inproj",
    )(jnp.asarray(src), u, wt_in)


def _qup_kernel(h_ref, g_ref, w_ref, cs_ref, o_ref):
    hn = _rms(h_ref[...].astype(F32), g_ref[...]).astype(BF16)
    qq = _dot(hn, w_ref[...]) * (MLA_SCALE * LOG2E)
    cs = cs_ref[...]
    for h in range(MLA_HEADS):
        o_ref[:, h * MLA_QK:h * MLA_QK + MLA_NOPE] = qq[:, h * 128:(h + 1) * 128].astype(BF16)
        t = qq[:, MLA_W + h * 128:MLA_W + (h + 1) * 128] * cs
        o_ref[:, h * MLA_QK + MLA_NOPE:(h + 1) * MLA_QK] = (t + pltpu.roll(t, 64, 1)).astype(BF16)


def _qup(p, col, g_q, w_q, cs, n_seq, tm):
    t = p.shape[0]
    spt = n_seq // tm
    return pl.pallas_call(
        _qup_kernel,
        out_shape=jax.ShapeDtypeStruct((t, MLA_HEADS * MLA_QK), BF16),
        grid=(t // tm,),
        in_specs=[pl.BlockSpec((tm, Q_LORA), lambda i: (i, col // Q_LORA)),
                  pl.BlockSpec((1, Q_LORA), lambda i: (0, 0)),
                  pl.BlockSpec((Q_LORA, 2 * MLA_W), lambda i: (0, 0)),
                  pl.BlockSpec((tm, 128), lambda i: (i % spt, 0))],
        out_specs=pl.BlockSpec((tm, MLA_HEADS * MLA_QK), lambda i: (i, 0)),
        compiler_params=_cparams("arbitrary"),
        name="mla_qup",
    )(p, g_q, w_q, cs)


def _kvup_kernel(p_ref, g_ref, wk_ref, wvt_ref, cs_ref, k_ref, vt_ref):
    p = p_ref[...]
    cn = _rms(p[:, :KV_LORA].astype(F32), g_ref[...]).astype(BF16)
    kn = _dot(cn, wk_ref[...])
    t = p[:, KV_LORA:KV_LORA + 128].astype(F32) * cs_ref[...]
    lane = lax.broadcasted_iota(jnp.int32, t.shape, 1)
    kr = jnp.where(lane < MLA_ROPE, t + pltpu.roll(t, 64, 1), 0.0).astype(BF16)
    for h in range(MLA_HEADS):
        k_ref[:, h * MLA_QK:h * MLA_QK + MLA_NOPE] = kn[:, h * 128:(h + 1) * 128].astype(BF16)
        k_ref[:, h * MLA_QK + MLA_NOPE:(h + 1) * MLA_QK] = kr
    vt_ref[0] = _dot_nt(wvt_ref[...], cn).astype(BF16)


def _kvup(p, col, g_kv, w_k, w_vt, cs, n_seq, tm):
    t = p.shape[0]
    spt = n_seq // tm
    return pl.pallas_call(
        _kvup_kernel,
        out_shape=(jax.ShapeDtypeStruct((t, MLA_HEADS * MLA_QK), BF16),
                   jax.ShapeDtypeStruct((t // n_seq, MLA_W, n_seq), BF16)),
        grid=(t // tm,),
        in_specs=[pl.BlockSpec((tm, KV_SEG), lambda i: (i, col // KV_SEG)),
                  pl.BlockSpec((1, KV_LORA), lambda i: (0, 0)),
                  pl.BlockSpec((KV_LORA, MLA_W), lambda i: (0, 0)),
                  pl.BlockSpec((MLA_W, KV_LORA), lambda i: (0, 0)),
                  pl.BlockSpec((tm, 128), lambda i: (i % spt, 0))],
        out_specs=(pl.BlockSpec((tm, MLA_HEADS * MLA_QK), lambda i: (i, 0)),
                   pl.BlockSpec((1, MLA_W, tm), lambda i: (i // spt, 0, i % spt))),
        compiler_params=_cparams("arbitrary"),
        name="mla_kvup",
    )(p, g_kv, w_k, w_vt, cs)


MLA_ROWS = 512
MLA_KEYS = 256
MLA_HPS = 2


def _mla_attn_kernel(q_ref, kc_ref, vtc_ref, *rest, has_lat):
    if has_lat:
        kl_ref, vtl_ref, o_ref = rest
    else:
        (o_ref,) = rest
    for hd in range(MLA_HPS):
        qk = slice(hd * MLA_QK, (hd + 1) * MLA_QK)
        vv = slice(hd * MLA_V, (hd + 1) * MLA_V)
        for r0 in range(0, q_ref.shape[0], MLA_ROWS):
            rows = slice(r0, r0 + MLA_ROWS)
            q = q_ref[rows, qk]
            st_c = _dot_nt(kc_ref[:, qk], q)
            m = jnp.max(st_c, axis=0, keepdims=True)
            st_l = []
            if has_lat:
                for k0 in range(0, kl_ref.shape[0], MLA_KEYS):
                    st = _dot_nt(kl_ref[k0:k0 + MLA_KEYS, qk], q)
                    m = jnp.maximum(m, jnp.max(st, axis=0, keepdims=True))
                    st_l.append(st)
            pt_c = jnp.exp2(st_c - m)
            den = jnp.sum(pt_c, axis=0, keepdims=True)
            ot = _dot(vtc_ref[0, vv, :], pt_c.astype(BF16))
            for i, st in enumerate(st_l):
                pt = jnp.exp2(st - m)
                den = den + jnp.sum(pt, axis=0, keepdims=True)
                ot = ot + _dot(vtl_ref[0, vv, i * MLA_KEYS:(i + 1) * MLA_KEYS], pt.astype(BF16))
            o_ref[rows, vv] = (ot / den).T.astype(o_ref.dtype)


def _mla_attn(q, kc, vtc, kl, vtl, n_batch, n_q, n_ctx, tq):
    has_lat = kl is not None
    qt = n_q // tq
    qkw, vw = MLA_HPS * MLA_QK, MLA_HPS * MLA_V
    in_specs = [pl.BlockSpec((tq, qkw), lambda b, h, i: (b * qt + i, h)),
                pl.BlockSpec((n_ctx, qkw), lambda b, h, i: (b, h)),
                pl.BlockSpec((1, vw, n_ctx), lambda b, h, i: (b, h, 0))]
    args = [q, kc, vtc]
    if has_lat:
        in_specs += [pl.BlockSpec((n_q, qkw), lambda b, h, i: (b, h)),
                     pl.BlockSpec((1, vw, n_q), lambda b, h, i: (b, h, 0))]
        args += [kl, vtl]
    return pl.pallas_call(
        functools.partial(_mla_attn_kernel, has_lat=has_lat),
        out_shape=jax.ShapeDtypeStruct((n_batch * n_q, MLA_W), BF16),
        grid=(n_batch, MLA_HEADS // MLA_HPS, qt),
        in_specs=in_specs,
        out_specs=pl.BlockSpec((tq, vw), lambda b, h, i: (b * qt + i, h)),
        compiler_params=_cparams("arbitrary", "arbitrary", "arbitrary"),
        name="mla_attn_lat" if has_lat else "mla_attn_ctx",
    )(*args)


def _pair_rows(q2):
    lane = lax.broadcasted_iota(jnp.int32, q2.shape, 1)
    zero = jnp.zeros_like(q2)
    return jnp.concatenate([jnp.where(lane < NA_HD, q2, zero), jnp.where(lane >= NA_HD, q2, zero)], axis=0)


def _unpair_rows(o, n):
    lane = lax.broadcasted_iota(jnp.int32, (n, 128), 1)
    return jnp.where(lane < NA_HD, o[:n], o[n:])


NA_G = 4
NA_SLAB = NA_G + NA_KH
NA_NEG_ENTRY = 2 * NA_KH
NA_GPS = 2


def _na_lat_kernel(q_ref, k_ref, v_ref, kc_ref, vc_ref, b_ref, o_ref):
    for gi in range(NA_GPS):
        qrows = slice(gi * NA_G * GRID_W, (gi + 1) * NA_G * GRID_W)
        _na_group(pl.program_id(1) * NA_GPS + gi, qrows, q_ref, k_ref, v_ref, kc_ref, vc_ref, b_ref, o_ref)


def _na_group(g, qrows, q_ref, k_ref, v_ref, kc_ref, vc_ref, b_ref, o_ref):
    rows = k_ref.shape[0] // GRID_W
    nq = NA_G * GRID_W
    us = jnp.clip(NA_G * g - NA_KH // 2, 0, rows - NA_SLAB)
    start = pl.multiple_of(us * GRID_W, GRID_W)

    entry, ok_l, ok_r = {}, {}, {}
    for dr in range(NA_G):
        r = NA_G * g + dr
        rs = jnp.clip(r - NA_KH // 2, 0, rows - NA_KH)
        for p in range(NA_SLAB // 2):
            kr = us + 2 * p
            l_ok = jnp.logical_and(kr >= rs, kr < rs + NA_KH)
            r_ok = jnp.logical_and(kr + 1 >= rs, kr + 1 < rs + NA_KH)
            ro = kr - r + (NA_KH - 1)
            entry[dr, p] = jnp.where(jnp.logical_or(l_ok, r_ok), jnp.clip(ro + 1, 0, NA_NEG_ENTRY - 1),
                                     NA_NEG_ENTRY)
            ok_l[dr, p] = l_ok.astype(jnp.int32)
            ok_r[dr, p] = r_ok.astype(jnp.int32)
    lane = lax.broadcasted_iota(jnp.int32, (GRID_W, 128), 1)

    def bias_tile(hp, sub, dr, p):
        t = b_ref[hp, entry[dr, p], sub * GRID_W:(sub + 1) * GRID_W, :]
        if dr % 2 == 1 and p in ((dr - 1) // 2, (dr + NA_KH - 1) // 2):
            t = jnp.where(jnp.where(lane < GRID_W, ok_l[dr, p], ok_r[dr, p]) > 0, t, NEG)
        return t

    for hp in range(NA_HEADS // 2):
        cols = slice(hp * 128, (hp + 1) * 128)
        qm = _pair_rows(q_ref[qrows, cols] * NA_SCALE)
        k2 = k_ref[pl.ds(start, NA_SLAB * GRID_W), cols]
        v2 = v_ref[pl.ds(start, NA_SLAB * GRID_W), cols]
        bias = jnp.concatenate(
            [jnp.concatenate([bias_tile(hp, sub, dr, p) for p in range(NA_SLAB // 2)], axis=1)
             for sub in range(2) for dr in range(NA_G)], axis=0)
        s_nb = _dot_nt(qm, k2) + bias
        s_cx = _dot_nt(qm, kc_ref[:, cols])
        m = jnp.maximum(jnp.max(s_nb, axis=-1, keepdims=True), jnp.max(s_cx, axis=-1, keepdims=True))
        p_nb = jnp.exp(s_nb - m)
        p_cx = jnp.exp(s_cx - m)
        den = jnp.sum(p_nb, axis=-1, keepdims=True) + jnp.sum(p_cx, axis=-1, keepdims=True)
        o = (_dot(p_nb.astype(BF16), v2) + _dot(p_cx.astype(BF16), vc_ref[:, cols])) / den
        o_ref[qrows, cols] = _unpair_rows(o, nq).astype(o_ref.dtype)


def _na_lat(p, pc, cq, ck, cv, cck, ccv, bias, n_batch, n_seq, n_ctx):
    rows = n_seq // GRID_W
    assert rows % (NA_G * NA_GPS) == 0 and rows >= NA_SLAB
    groups = rows // (NA_G * NA_GPS)
    nq = NA_GPS * NA_G * GRID_W
    return pl.pallas_call(
        _na_lat_kernel,
        out_shape=jax.ShapeDtypeStruct((n_batch * n_seq, NA_W), BF16),
        grid=(n_batch, groups),
        in_specs=[pl.BlockSpec((nq, NA_W), lambda b, g: (b * groups + g, cq // NA_W)),
                  pl.BlockSpec((n_seq, NA_W), lambda b, g: (b, ck // NA_W)),
                  pl.BlockSpec((n_seq, NA_W), lambda b, g: (b, cv // NA_W)),
                  pl.BlockSpec((n_ctx, NA_W), lambda b, g: (b, cck // NA_W)),
                  pl.BlockSpec((n_ctx, NA_W), lambda b, g: (b, ccv // NA_W)),
                  pl.BlockSpec((NA_HEADS // 2, NA_NEG_ENTRY + 1, 2 * GRID_W, 2 * GRID_W),
                               lambda b, g: (0, 0, 0, 0), pipeline_mode=pl.Buffered(1))],
        out_specs=pl.BlockSpec((nq, NA_W), lambda b, g: (b * groups + g, 0)),
        compiler_params=_cparams("arbitrary", "arbitrary"),
        name="na_lat",
    )(p, p, p, pc, pc, bias)


def _na_ctx_kernel(q_ref, kc_ref, vc_ref, o_ref):
    n = q_ref.shape[0]
    for hp in range(NA_HEADS // 2):
        cols = slice(hp * 128, (hp + 1) * 128)
        qm = _pair_rows(q_ref[:, cols] * NA_SCALE)
        s = _dot_nt(qm, kc_ref[:, cols])
        p = jnp.exp(s - jnp.max(s, axis=-1, keepdims=True))
        o = _dot(p.astype(BF16), vc_ref[:, cols]) / jnp.sum(p, axis=-1, keepdims=True)
        o_ref[:, cols] = _unpair_rows(o, n).astype(o_ref.dtype)


def _na_ctx(pc, cq, ck, cv, n_batch, n_ctx):
    return pl.pallas_call(
        _na_ctx_kernel,
        out_shape=jax.ShapeDtypeStruct((n_batch * n_ctx, NA_W), BF16),
        grid=(n_batch,),
        in_specs=[pl.BlockSpec((n_ctx, NA_W), lambda b: (b, cq // NA_W)),
                  pl.BlockSpec((n_ctx, NA_W), lambda b: (b, ck // NA_W)),
                  pl.BlockSpec((n_ctx, NA_W), lambda b: (b, cv // NA_W))],
        out_specs=pl.BlockSpec((n_ctx, NA_W), lambda b: (b, 0)),
        compiler_params=_cparams("arbitrary"),
        name="na_ctx",
    )(pc, pc, pc)


def _na_bias_table(rpb):
    c = np.arange(GRID_W)[:, None]
    kc = np.arange(GRID_W)[None, :]
    cs = np.clip(c - NA_KW // 2, 0, GRID_W - NA_KW)
    valid = (kc >= cs) & (kc < cs + NA_KW)
    co = kc - c + (NA_KW - 1)
    onehot = (valid[None] & (co[None] == np.arange(2 * NA_KW - 1)[:, None, None])).astype(np.float32)
    t = jnp.einsum('hrj,jck->hrck', rpb.astype(F32), onehot, precision=lax.Precision.HIGHEST)
    t = t + np.where(valid, 0.0, NEG).astype(np.float32)
    t = t.reshape(NA_HEADS // 2, 2, 2 * NA_KH - 1, GRID_W, GRID_W).transpose(0, 2, 1, 3, 4)
    t = t.reshape(NA_HEADS // 2, 2 * NA_KH - 1, 2 * GRID_W, GRID_W)
    neg = jnp.full((NA_HEADS // 2, 1, 2 * GRID_W, GRID_W), NEG, F32)
    t = jnp.concatenate([neg, t, neg, neg], axis=1)
    return jnp.concatenate([t[:, :-1], t[:, 1:]], axis=-1)


def _fourier_kernel(x_ref, cg_ref, sg_ref, ch_ref, sh_ref, j_ref, o_ref, *, norm):
    x = x_ref[...]
    n = x.shape[0]
    h = n // 2
    rb = j_ref.shape[0]
    jm = j_ref[...]

    def rev(t):
        m = t.shape[0]
        return jnp.concatenate([_dot(jm, t[m - (i + 1) * rb:m - i * rb, :]) for i in range(m // rb)], axis=0)

    def mirror(t, first_row):
        row = lax.broadcasted_iota(jnp.int32, t.shape, 0)
        return jnp.where(row == 0, first_row, pltpu.roll(rev(t), 1, 0))

    a = _dot(x, cg_ref[...])
    b = _dot(x, sg_ref[...])
    ta = mirror(a.astype(BF16), 0.0)
    tb = mirror(b.astype(BF16), 0.0)
    ae = (a + ta)[:h].astype(BF16)
    bo = (b - tb)[:h].astype(BF16)
    rowh = lax.broadcasted_iota(jnp.int32, (h, x.shape[1]), 0)
    sign_h = jnp.where(rowh % 2 == 0, 1.0, -1.0)
    e = _dot(ch_ref[...], ae) + sign_h * a[h:h + 1, :]
    o = _dot(sh_ref[...], bo)
    o_ref[:h, :] = ((e - o) * norm).astype(o_ref.dtype)
    rown = lax.broadcasted_iota(jnp.int32, a.shape, 0)
    mid = jnp.sum(jnp.where(rown % 2 == 0, a, -a), axis=0, keepdims=True)
    g = ((e + o) * norm).astype(BF16)
    o_ref[h:, :] = mirror(g, mid * norm).astype(o_ref.dtype)


@functools.lru_cache(maxsize=None)
def _dft_tables(n):
    j = np.arange(n, dtype=np.int64)
    ang = ((j[:, None] * j[None, :]) % n).astype(np.float64) * (2.0 * np.pi / n)
    return np.cos(ang).astype(np.float32), np.sin(ang).astype(np.float32)


def _fourier(p, n_batch, n_seq):
    h = n_seq // 2
    rb = min(256, h)
    assert h % rb == 0
    cg, sg = (jnp.asarray(t).astype(BF16) for t in _dft_tables(FNET_GW))
    ch, sh = (jnp.asarray(np.ascontiguousarray(t[:h, :h])).astype(BF16) for t in _dft_tables(n_seq))
    flip = jnp.asarray(np.eye(rb, dtype=np.float32)[::-1].copy()).astype(BF16)
    const = lambda b, g: (0, 0)
    return pl.pallas_call(
        functools.partial(_fourier_kernel, norm=float((n_seq * FNET_GW) ** -0.5)),
        out_shape=jax.ShapeDtypeStruct((n_batch * n_seq, FNET_W), BF16),
        grid=(n_batch, FNET_GROUPS),
        in_specs=[pl.BlockSpec((n_seq, FNET_GW), lambda b, g: (b, C_FV // FNET_GW + g)),
                  pl.BlockSpec((FNET_GW, FNET_GW), const),
                  pl.BlockSpec((FNET_GW, FNET_GW), const),
                  pl.BlockSpec((h, h), const, pipeline_mode=pl.Buffered(1)),
                  pl.BlockSpec((h, h), const, pipeline_mode=pl.Buffered(1)),
                  pl.BlockSpec((rb, rb), const)],
        out_specs=pl.BlockSpec((n_seq, FNET_GW), lambda b, g: (b, g)),
        compiler_params=_cparams("arbitrary", "arbitrary"),
        name="fourier",
    )(p, cg, sg, ch, sh, flip)


HALO = 16


def _combine_kernel(cbx_ref, ccp_ref, cxp_ref, ccn_ref, cxn_ref, cw_ref, o1_ref, o2_ref, o3_ref, g_ref, mg_ref,
                    w_ref, m_ref, *, tiles_per_seq):
    tm = m_ref.shape[0]
    pos = pl.program_id(0) % tiles_per_seq
    z = cbx_ref[:, CONV_W:2 * CONV_W].astype(F32) * cbx_ref[:, 2 * CONV_W:].astype(F32)
    z_before = ccp_ref[HALO - 1:HALO, :].astype(F32) * cxp_ref[HALO - 1:HALO, :].astype(F32)
    z_after = ccn_ref[0:1, :].astype(F32) * cxn_ref[0:1, :].astype(F32)
    z_before = jnp.where(pos == 0, 0.0, z_before)
    z_after = jnp.where(pos == tiles_per_seq - 1, 0.0, z_after)
    row = lax.broadcasted_iota(jnp.int32, z.shape, 0)
    zp = jnp.where(row == 0, z_before, pltpu.roll(z, 1, 0))
    zn = jnp.where(row == tm - 1, z_after, pltpu.roll(z, tm - 1, 0))
    cw = cw_ref[...]
    conv_o = cbx_ref[:, :CONV_W].astype(F32) * (zp * cw[0:1] + z * cw[1:2] + zn * cw[2:3])

    acc = None
    for k in range(N_BRANCH):
        o = conv_o if k == 0 else (o1_ref, o2_ref, o3_ref)[k - 1][...].astype(F32)
        g = g_ref[:, k * 1024:(k + 1) * 1024].astype(F32)
        t = (o * _silu(g)).astype(BF16)
        gate = jax.nn.sigmoid(mg_ref[:, k * D_MODEL:(k + 1) * D_MODEL].astype(F32))
        r = _dot(t, w_ref[k]) * gate
        acc = r if acc is None else acc + r
    m_ref[...] = acc.astype(m_ref.dtype)


def _combine(p, outs, conv_w, w_p, n_seq, tm):
    t = p.shape[0]
    assert n_seq % tm == 0 and tm % HALO == 0
    hb = tm // HALO
    last = t // HALO - 1
    o_spec = pl.BlockSpec((tm, 1024), lambda i: (i, 0))

    def halo(col, nxt):
        if nxt:
            return pl.BlockSpec((HALO, CONV_W), lambda i: (jnp.minimum((i + 1) * hb, last), col // CONV_W))
        return pl.BlockSpec((HALO, CONV_W), lambda i: (jnp.maximum(i * hb - 1, 0), col // CONV_W))

    return pl.pallas_call(
        functools.partial(_combine_kernel, tiles_per_seq=n_seq // tm),
        out_shape=jax.ShapeDtypeStruct((t, D_MODEL), BF16),
        grid=(t // tm,),
        in_specs=[pl.BlockSpec((tm, 3 * CONV_W), lambda i: (i, C_CB // (3 * CONV_W))),
                  halo(C_CC, False), halo(C_CX, False), halo(C_CC, True), halo(C_CX, True),
                  pl.BlockSpec((CONV_K, CONV_W), lambda i: (0, 0)),
                  o_spec, o_spec, o_spec,
                  pl.BlockSpec((tm, N_BRANCH * 1024), lambda i: (i, C_G // (N_BRANCH * 1024))),
                  pl.BlockSpec((tm, N_BRANCH * D_MODEL), lambda i: (i, C_MG // (N_BRANCH * D_MODEL))),
                  pl.BlockSpec((N_BRANCH, 1024, D_MODEL), lambda i: (0, 0, 0), pipeline_mode=pl.Buffered(1))],
        out_specs=pl.BlockSpec((tm, D_MODEL), lambda i: (i, 0)),
        compiler_params=_cparams("arbitrary"),
        name="combine",
    )(p, p, p, p, p, conv_w, *outs, p, p, w_p)


OUT_ROWS = 256


def _outproj_kernel(m_ref, w_ref, x_ref, gt_ref, g_ref, *rest, with_next):
    if with_next:
        sh_ref, sc_ref, gn_ref, o_ref, u_ref = rest
    else:
        (o_ref,) = rest
    for r0 in range(0, m_ref.shape[0], OUT_ROWS):
        rows = slice(r0, r0 + OUT_ROWS)
        y = _dot(m_ref[rows, :], w_ref[...])
        x_new = x_ref[rows, :] + gt_ref[0] * _rms(y, g_ref[...])
        if with_next:
            u_ref[rows, :] = (_rms(x_new, gn_ref[...]) * (1.0 + sc_ref[0]) + sh_ref[0]).astype(BF16)
        o_ref[rows, :] = x_new


def _outproj(m, w_out, x2, mod3, row_of_tile, g_post, tm, nxt=None):
    t = m.shape[0]
    tile = pl.BlockSpec((tm, D_MODEL), lambda i: (i, 0))
    vec = pl.BlockSpec((1, D_MODEL), lambda i: (0, 0))

    def mod_spec(part):
        return pl.BlockSpec((1, 1, D_MODEL), lambda i: (row_of_tile(i), 0, part))

    in_specs = [tile, pl.BlockSpec((D_MODEL, D_MODEL), lambda i: (0, 0), pipeline_mode=pl.Buffered(1)),
                tile, mod_spec(2), vec]
    args = [m, w_out, x2, mod3, g_post]
    out_shape = jax.ShapeDtypeStruct((t, D_MODEL), F32)
    out_specs = tile
    if nxt is not None:
        in_specs += [mod_spec(0), mod_spec(1), vec]
        args += [nxt[0], nxt[0], nxt[1]]
        out_shape = (out_shape, jax.ShapeDtypeStruct((t, D_MODEL), BF16))
        out_specs = (tile, tile)
    return pl.pallas_call(
        functools.partial(_outproj_kernel, with_next=nxt is not None),
        out_shape=out_shape,
        grid=(t // tm,),
        in_specs=in_specs,
        out_specs=out_specs,
        compiler_params=_cparams("arbitrary"),
        name="outproj",
    )(*args)


def _rope_swap(w):
    return jnp.concatenate([-w[..., 16:32], w[..., 0:16], -w[..., 48:64], w[..., 32:48]], axis=-1)


def _prep_w_uq(w):
    w3 = w.reshape(Q_LORA, MLA_HEADS, MLA_NOPE + MLA_ROPE)
    nope = w3[:, :, :MLA_NOPE].reshape(Q_LORA, MLA_W)
    r = w3[:, :, MLA_NOPE:]
    rr = jnp.concatenate([r, _rope_swap(r)], axis=-1).reshape(Q_LORA, MLA_HEADS * 128)
    return jnp.concatenate([nope, rr], axis=1).astype(BF16)


def _prep_w_ukv(w):
    w3 = w.reshape(KV_LORA, MLA_HEADS, MLA_NOPE + MLA_V)
    return (w3[:, :, :MLA_NOPE].reshape(KV_LORA, MLA_W).astype(BF16),
            w3[:, :, MLA_NOPE:].reshape(KV_LORA, MLA_W).T.astype(BF16))


def _rope_table(n):
    nf = MLA_ROPE // 4
    t = jnp.arange(n, dtype=jnp.int32)
    pos = jnp.stack([t // GRID_W, t % GRID_W], axis=-1).astype(F32)
    inv = ROPE_THETA ** (-jnp.arange(nf, dtype=F32) / nf)
    ang = pos[:, :, None] * inv
    cos = jnp.repeat(jnp.cos(ang), 2, axis=1).reshape(n, MLA_ROPE)
    sin = jnp.repeat(jnp.sin(ang), 2, axis=1).reshape(n, MLA_ROPE)
    return jnp.concatenate([cos, sin], axis=-1).astype(F32)


def _identity_rope_table(n):
    return jnp.concatenate([jnp.ones((n, MLA_ROPE), F32), jnp.zeros((n, MLA_ROPE), F32)], axis=-1)


def kernel(x, c, ctx, c_ctx, g_pre, g_post, w_ada, b_ada, w_in, g_q, g_kv, w_uq, w_ukv, conv_w, na_rpb,
           w_p_conv, w_p_mla, w_p_na, w_p_fnet, w_out):
    n_batch, n_seq, _ = x.shape
    n_ctx = ctx.shape[1]
    assert n_batch + 1 <= 8 and n_seq % 1024 == 0 and n_ctx % 256 == 0
    xl = x.reshape(n_batch * n_seq, D_MODEL)
    xc = ctx.reshape(n_batch * n_ctx, D_MODEL)

    cc = jnp.concatenate([c, c_ctx[None], jnp.zeros((8 - n_batch - 1, D_MODEL), c.dtype)], axis=0)
    mod = _ada(cc, w_ada, b_ada.reshape(DEPTH, 1, 3 * D_MODEL))

    wt_in = jnp.swapaxes(w_in, 1, 2)
    rope_l = _rope_table(n_seq)
    rope_c = _identity_rope_table(n_ctx)
    tm_l = min(2048, n_batch * n_seq)
    tm_c = n_batch * n_ctx
    ctx_row = lambda i: n_batch

    def lat_row_t(tm):
        return lambda i: i * tm // n_seq

    for l in range(DEPTH):
        ctx_out = l < DEPTH - 1
        mod3 = mod[l].reshape(8, 1, 3 * D_MODEL)
        w_q = _prep_w_uq(w_uq[l])
        w_k, w_vt = _prep_w_ukv(w_ukv[l])
        gq = g_q[l].reshape(1, Q_LORA)
        gkv = g_kv[l].reshape(1, KV_LORA)
        gpre = g_pre[l].reshape(1, D_MODEL)
        gpost = g_post[l].reshape(1, D_MODEL)
        w_p = jnp.stack([w_p_conv[l], w_p_mla[l], w_p_na[l], w_p_fnet[l]]).astype(BF16)
        w_o = w_out[l].astype(BF16)
        bias = _na_bias_table(na_rpb[l])

        all_tiles = tuple(range(NP_FULL // TN))
        if l == 0:
            ul = _prenorm(xl, mod3, lat_row_t(512), gpre, 512)
            uc = _prenorm(xc, mod3, ctx_row, gpre, min(512, tm_c))
        nxt = None
        if l + 1 < DEPTH:
            nxt = (mod[l + 1].reshape(8, 1, 3 * D_MODEL), g_pre[l + 1].reshape(1, D_MODEL))
        pl_ = _inproj(ul, wt_in, l, all_tiles, tm_l)
        if ctx_out:
            pc = _inproj(uc, wt_in, l, all_tiles, tm_c)
            base = 0
        else:
            pc = _inproj(uc, wt_in, l, all_tiles[KV_BASE // TN:], tm_c)
            base = KV_BASE

        kc_m, vc_m = _kvup(pc, C_KV - base, gkv, w_k, w_vt, rope_c, n_ctx, n_ctx)
        kl_m, vl_m = _kvup(pl_, C_KV, gkv, w_k, w_vt, rope_l, n_seq, 1024)
        ql_m = _qup(pl_, C_QC, gq, w_q, rope_l, n_seq, 1024)
        mla_l = _mla_attn(ql_m, kc_m, vc_m, kl_m, vl_m, n_batch, n_seq, n_ctx, min(2048, n_seq))
        na_l = _na_lat(pl_, pc, C_NQ, C_NK, C_NV, C_NK - base, C_NV - base, bias, n_batch, n_seq, n_ctx)
        fn_l = _fourier(pl_, n_batch, n_seq)
        m_l = _combine(pl_, (mla_l, na_l, fn_l), conv_w[l], w_p, n_seq, 256)
        xl_new = _outproj(m_l, w_o, xl, mod3, lat_row_t(512), gpost, 512, nxt)
        if nxt is not None:
            xl_new, ul = xl_new

        if ctx_out:
            qc_m = _qup(pc, C_QC, gq, w_q, rope_c, n_ctx, n_ctx)
            mla_c = _mla_attn(qc_m, kc_m, vc_m, None, None, n_batch, n_ctx, n_ctx, n_ctx)
            na_c = _na_ctx(pc, C_NQ, C_NK, C_NV, n_batch, n_ctx)
            fn_c = _fourier(pc, n_batch, n_ctx)
            m_c = _combine(pc, (mla_c, na_c, fn_c), conv_w[l], w_p, n_ctx, 256)
            xc, uc = _outproj(m_c, w_o, xc, mod3, ctx_row, gpost, min(512, tm_c), nxt)
        xl = xl_new

    return xl.reshape(n_batch, n_seq, D_MODEL)
```

```python
import functools

import jax
import jax.numpy as jnp
import numpy as np
from jax import lax
from jax.experimental import pallas as pl
from jax.experimental.pallas import tpu as pltpu

F32 = jnp.float32
BF16 = jnp.bfloat16

D_MODEL = 2048
DEPTH = 2
GRID_W = 64
CONV_W = 1024
CONV_K = 3
MLA_HEADS = 8
MLA_NOPE = 128
MLA_ROPE = 64
MLA_V = 128
Q_LORA = 512
KV_LORA = 256
MLA_W = MLA_HEADS * MLA_V
MLA_SCALE = (MLA_NOPE + MLA_ROPE) ** -0.5
LOG2E = 1.4426950408889634
NA_HEADS = 16
NA_HD = 64
NA_W = NA_HEADS * NA_HD
NA_KH = 8
NA_KW = 16
NA_SCALE = NA_HD ** -0.5
FNET_GROUPS = 4
FNET_GW = 256
FNET_W = FNET_GROUPS * FNET_GW
N_BRANCH = 4
ROPE_THETA = 10000.0
EPS = 1e-6
NEG = -1e30

VMEM_LIMIT_BYTES = 56 * 1024 * 1024

MLA_QK = 256
KV_SEG = 512
C_MG, C_G, C_CB, C_CC, C_CX, C_FV, C_NQ, C_NK, C_NV, C_QC, C_KV = (
    0, 8192, 12288, 13312, 14336, 15360, 16384, 17408, 18432, 19456, 19968)
NP_FULL = 20480
KV_BASE = C_NK


def _cparams(*sem):
    return pltpu.CompilerParams(dimension_semantics=sem, vmem_limit_bytes=VMEM_LIMIT_BYTES)


def _rms(x, g):
    return x * lax.rsqrt(jnp.mean(x * x, axis=-1, keepdims=True) + EPS) * g


def _silu(x):
    return x * jax.nn.sigmoid(x)


def _dot(a, b):
    return jnp.dot(a, b, preferred_element_type=F32)


def _dot_nt(a, b):
    return lax.dot_general(a, b, (((1,), (1,)), ((), ())), preferred_element_type=F32)


def _ada_kernel(c_ref, w_ref, b_ref, o_ref):
    a = _silu(c_ref[...]).astype(BF16)
    o_ref[0] = _dot(a, w_ref[0].astype(BF16)) + b_ref[0]


def _ada(cc, w_ada, b_ada):
    tn = 1024
    return pl.pallas_call(
        _ada_kernel,
        out_shape=jax.ShapeDtypeStruct((DEPTH, 8, 3 * D_MODEL), F32),
        grid=(DEPTH, 3 * D_MODEL // tn),
        in_specs=[pl.BlockSpec((8, D_MODEL), lambda l, j: (0, 0)),
                  pl.BlockSpec((1, D_MODEL, tn), lambda l, j: (l, 0, j)),
                  pl.BlockSpec((1, 1, tn), lambda l, j: (l, 0, j))],
        out_specs=pl.BlockSpec((1, 8, tn), lambda l, j: (l, 0, j)),
        compiler_params=_cparams("arbitrary", "arbitrary"),
        name="ada",
    )(cc, w_ada, b_ada)


def _prenorm_kernel(x_ref, sh_ref, sc_ref, g_ref, u_ref):
    y = _rms(x_ref[...], g_ref[...])
    u_ref[...] = (y * (1.0 + sc_ref[0]) + sh_ref[0]).astype(BF16)


def _prenorm(x2, mod3, row_of_tile, g_pre, tm):
    t = x2.shape[0]
    return pl.pallas_call(
        _prenorm_kernel,
        out_shape=jax.ShapeDtypeStruct((t, D_MODEL), BF16),
        grid=(t // tm,),
        in_specs=[pl.BlockSpec((tm, D_MODEL), lambda i: (i, 0)),
                  pl.BlockSpec((1, 1, D_MODEL), lambda i: (row_of_tile(i), 0, 0)),
                  pl.BlockSpec((1, 1, D_MODEL), lambda i: (row_of_tile(i), 0, 1)),
                  pl.BlockSpec((1, D_MODEL), lambda i: (0, 0))],
        out_specs=pl.BlockSpec((tm, D_MODEL), lambda i: (i, 0)),
        compiler_params=_cparams("arbitrary"),
        name="prenorm",
    )(x2, mod3, mod3, g_pre)


TN = 1024
W_ROWS = 256
N_ORIG = 20288

_TILE_SRC = ([12096 + TN * t for t in range(8)] + [8000 + TN * t for t in range(4)]
             + [3904 + TN * t for t in range(3)] + [6976, 2880, 320, 1344, -1])
SRC_QC = 2368
KV_ROWS = KV_LORA + MLA_ROPE


def _inproj_kernel(src_ref, u_ref, wt_hbm, o_ref, stage, wbuf, sems, *, layer, n_tiles):
    j = pl.program_id(0)

    def main_copy(jj):
        start = pl.multiple_of(src_ref[jj], 8)
        return pltpu.make_async_copy(wt_hbm.at[layer, pl.ds(start, TN), :], stage, sems.at[0])

    def tail_copies():
        return (pltpu.make_async_copy(wt_hbm.at[layer, pl.ds(SRC_QC, Q_LORA), :],
                                      stage.at[pl.ds(0, Q_LORA), :], sems.at[0]),
                pltpu.make_async_copy(wt_hbm.at[layer, pl.ds(0, KV_ROWS), :],
                                      stage.at[pl.ds(Q_LORA, KV_ROWS), :], sems.at[1]))

    def start(jj):
        @pl.when(src_ref[jj] >= 0)
        def _():
            main_copy(jj).start()

        @pl.when(src_ref[jj] < 0)
        def _():
            for c in tail_copies():
                c.start()

    def convert(n_rows):
        def body(r, carry):
            r0 = pl.multiple_of(r * W_ROWS, W_ROWS)
            wbuf[pl.ds(r0, W_ROWS), :] = stage[pl.ds(r0, W_ROWS), :].astype(BF16)
            return carry

        lax.fori_loop(0, n_rows // W_ROWS, body, 0)

    @pl.when(pl.program_id(1) == 0)
    def _():
        @pl.when(j == 0)
        def _():
            start(0)

        @pl.when(src_ref[j] >= 0)
        def _():
            main_copy(j).wait()
            convert(TN)

        @pl.when(src_ref[j] < 0)
        def _():
            for c in tail_copies():
                c.wait()
            convert(768)
            kr0 = Q_LORA + KV_LORA
            wbuf[768:832, :] = stage[768:832, :].astype(BF16)
            wbuf[832:848, :] = (-stage[kr0 + 16:kr0 + 32, :]).astype(BF16)
            wbuf[848:864, :] = stage[kr0:kr0 + 16, :].astype(BF16)
            wbuf[864:880, :] = (-stage[kr0 + 48:kr0 + 64, :]).astype(BF16)
            wbuf[880:896, :] = stage[kr0 + 32:kr0 + 48, :].astype(BF16)
            wbuf[896:TN, :] = jnp.zeros((TN - 896, D_MODEL), BF16)

        @pl.when(j + 1 < n_tiles)
        def _():
            start(j + 1)

    o_ref[...] = _dot_nt(u_ref[...], wbuf[...]).astype(o_ref.dtype)


def _inproj(u, wt_in, layer, tiles, tm):
    t = u.shape[0]
    src = np.array([_TILE_SRC[j] for j in tiles], np.int32)
    assert all(s % 8 == 0 and s + TN <= N_ORIG for s in src[src >= 0])
    grid_spec = pltpu.PrefetchScalarGridSpec(
        num_scalar_prefetch=1,
        grid=(len(tiles), t // tm),
        in_specs=[pl.BlockSpec((tm, D_MODEL), lambda j, i, s: (i, 0)),
                  pl.BlockSpec(memory_space=pl.ANY)],
        out_specs=pl.BlockSpec((tm, TN), lambda j, i, s: (i, j)),
        scratch_shapes=[pltpu.VMEM((TN, D_MODEL), F32), pltpu.VMEM((TN, D_MODEL), BF16),
                        pltpu.SemaphoreType.DMA((2,))])
    return pl.pallas_call(
        functools.partial(_inproj_kernel, layer=layer, n_tiles=len(tiles)),
        out_shape=jax.ShapeDtypeStruct((t, len(tiles) * TN), BF16),
        grid_spec=grid_spec,
        compiler_params=_cparams("arbitrary", "arbitrary"),
        name="inproj",
    )(jnp.asarray(src), u, wt_in)


def _qup_kernel(h_ref, g_ref, w_ref, cs_ref, o_ref):
    hn = _rms(h_ref[...].astype(F32), g_ref[...]).astype(BF16)
    qq = _dot(hn, w_ref[...]) * (MLA_SCALE * LOG2E)
    cs = cs_ref[...]
    for h in range(MLA_HEADS):
        o_ref[:, h * MLA_QK:h * MLA_QK + MLA_NOPE] = qq[:, h * 128:(h + 1) * 128].astype(BF16)
        t = qq[:, MLA_W + h * 128:MLA_W + (h + 1) * 128] * cs
        o_ref[:, h * MLA_QK + MLA_NOPE:(h + 1) * MLA_QK] = (t + pltpu.roll(t, 64, 1)).astype(BF16)


def _qup(p, col, g_q, w_q, cs, n_seq, tm):
    t = p.shape[0]
    spt = n_seq // tm
    return pl.pallas_call(
        _qup_kernel,
        out_shape=jax.ShapeDtypeStruct((t, MLA_HEADS * MLA_QK), BF16),
        grid=(t // tm,),
        in_specs=[pl.BlockSpec((tm, Q_LORA), lambda i: (i, col // Q_LORA)),
                  pl.BlockSpec((1, Q_LORA), lambda i: (0, 0)),
                  pl.BlockSpec((Q_LORA, 2 * MLA_W), lambda i: (0, 0)),
                  pl.BlockSpec((tm, 128), lambda i: (i % spt, 0))],
        out_specs=pl.BlockSpec((tm, MLA_HEADS * MLA_QK), lambda i: (i, 0)),
        compiler_params=_cparams("arbitrary"),
        name="mla_qup",
    )(p, g_q, w_q, cs)


def _kvup_kernel(p_ref, g_ref, wk_ref, wvt_ref, cs_ref, k_ref, vt_ref):
    p = p_ref[...]
    cn = _rms(p[:, :KV_LORA].astype(F32), g_ref[...]).astype(BF16)
    kn = _dot(cn, wk_ref[...])
    t = p[:, KV_LORA:KV_LORA + 128].astype(F32) * cs_ref[...]
    lane = lax.broadcasted_iota(jnp.int32, t.shape, 1)
    kr = jnp.where(lane < MLA_ROPE, t + pltpu.roll(t, 64, 1), 0.0).astype(BF16)
    for h in range(MLA_HEADS):
        k_ref[:, h * MLA_QK:h * MLA_QK + MLA_NOPE] = kn[:, h * 128:(h + 1) * 128].astype(BF16)
        k_ref[:, h * MLA_QK + MLA_NOPE:(h + 1) * MLA_QK] = kr
    vt_ref[0] = _dot_nt(wvt_ref[...], cn).astype(BF16)


def _kvup(p, col, g_kv, w_k, w_vt, cs, n_seq, tm):
    t = p.shape[0]
    spt = n_seq // tm
    return pl.pallas_call(
        _kvup_kernel,
        out_shape=(jax.ShapeDtypeStruct((t, MLA_HEADS * MLA_QK), BF16),
                   jax.ShapeDtypeStruct((t // n_seq, MLA_W, n_seq), BF16)),
        grid=(t // tm,),
        in_specs=[pl.BlockSpec((tm, KV_SEG), lambda i: (i, col // KV_SEG)),
                  pl.BlockSpec((1, KV_LORA), lambda i: (0, 0)),
                  pl.BlockSpec((KV_LORA, MLA_W), lambda i: (0, 0)),
                  pl.BlockSpec((MLA_W, KV_LORA), lambda i: (0, 0)),
                  pl.BlockSpec((tm, 128), lambda i: (i % spt, 0))],
        out_specs=(pl.BlockSpec((tm, MLA_HEADS * MLA_QK), lambda i: (i, 0)),
                   pl.BlockSpec((1, MLA_W, tm), lambda i: (i // spt, 0, i % spt))),
        compiler_params=_cparams("arbitrary"),
        name="mla_kvup",
    )(p, g_kv, w_k, w_vt, cs)


MLA_ROWS = 512
MLA_KEYS = 256
MLA_HPS = 2
MLA_AHEAD = 1


def _mla_attn_kernel(q_ref, kc_ref, vtc_ref, *rest, has_lat):
    if has_lat:
        kl_ref, vtl_ref, o_ref = rest
    else:
        (o_ref,) = rest
    def scores(hd, r0):
        qk = slice(hd * MLA_QK, (hd + 1) * MLA_QK)
        q = q_ref[r0:r0 + MLA_ROWS, qk]
        st_c = _dot_nt(kc_ref[:, qk], q)
        m = jnp.max(st_c, axis=0, keepdims=True)
        st_l = []
        if has_lat:
            for k0 in range(0, kl_ref.shape[0], MLA_KEYS):
                st = _dot_nt(kl_ref[k0:k0 + MLA_KEYS, qk], q)
                m = jnp.maximum(m, jnp.max(st, axis=0, keepdims=True))
                st_l.append(st)
        return st_c, st_l, m

    def finish(hd, r0, st_c, st_l, m):
        vv = slice(hd * MLA_V, (hd + 1) * MLA_V)
        pt_c = jnp.exp2(st_c - m)
        den = jnp.sum(pt_c, axis=0, keepdims=True)
        ot = _dot(vtc_ref[0, vv, :], pt_c.astype(BF16))
        for i, st in enumerate(st_l):
            pt = jnp.exp2(st - m)
            den = den + jnp.sum(pt, axis=0, keepdims=True)
            ot = ot + _dot(vtl_ref[0, vv, i * MLA_KEYS:(i + 1) * MLA_KEYS], pt.astype(BF16))
        o_ref[r0:r0 + MLA_ROWS, vv] = (ot / den).T.astype(o_ref.dtype)

    chains = [(hd, r0) for hd in range(MLA_HPS) for r0 in range(0, q_ref.shape[0], MLA_ROWS)]
    pending = []
    for chain in chains:
        pending.append(chain + scores(*chain))
        if len(pending) > MLA_AHEAD:
            finish(*pending.pop(0))
    for p in pending:
        finish(*p)


def _mla_attn(q, kc, vtc, kl, vtl, n_batch, n_q, n_ctx, tq):
    has_lat = kl is not None
    qt = n_q // tq
    qkw, vw = MLA_HPS * MLA_QK, MLA_HPS * MLA_V
    in_specs = [pl.BlockSpec((tq, qkw), lambda b, h, i: (b * qt + i, h)),
                pl.BlockSpec((n_ctx, qkw), lambda b, h, i: (b, h)),
                pl.BlockSpec((1, vw, n_ctx), lambda b, h, i: (b, h, 0))]
    args = [q, kc, vtc]
    if has_lat:
        in_specs += [pl.BlockSpec((n_q, qkw), lambda b, h, i: (b, h)),
                     pl.BlockSpec((1, vw, n_q), lambda b, h, i: (b, h, 0))]
        args += [kl, vtl]
    return pl.pallas_call(
        functools.partial(_mla_attn_kernel, has_lat=has_lat),
        out_shape=jax.ShapeDtypeStruct((n_batch * n_q, MLA_W), BF16),
        grid=(n_batch, MLA_HEADS // MLA_HPS, qt),
        in_specs=in_specs,
        out_specs=pl.BlockSpec((tq, vw), lambda b, h, i: (b * qt + i, h)),
        compiler_params=_cparams("arbitrary", "arbitrary", "arbitrary"),
        name="mla_attn_lat" if has_lat else "mla_attn_ctx",
    )(*args)


def _pair_rows(q2):
    lane = lax.broadcasted_iota(jnp.int32, q2.shape, 1)
    zero = jnp.zeros_like(q2)
    return jnp.concatenate([jnp.where(lane < NA_HD, q2, zero), jnp.where(lane >= NA_HD, q2, zero)], axis=0)


def _unpair_rows(o, n):
    lane = lax.broadcasted_iota(jnp.int32, (n, 128), 1)
    return jnp.where(lane < NA_HD, o[:n], o[n:])


NA_G = 4
NA_SLAB = NA_G + NA_KH
NA_NEG_ENTRY = 2 * NA_KH
NA_GPS = 2


def _na_lat_kernel(q_ref, k_ref, v_ref, kc_ref, vc_ref, b_ref, o_ref):
    for gi in range(NA_GPS):
        qrows = slice(gi * NA_G * GRID_W, (gi + 1) * NA_G * GRID_W)
        _na_group(pl.program_id(1) * NA_GPS + gi, qrows, q_ref, k_ref, v_ref, kc_ref, vc_ref, b_ref, o_ref)


def _na_group(g, qrows, q_ref, k_ref, v_ref, kc_ref, vc_ref, b_ref, o_ref):
    rows = k_ref.shape[0] // GRID_W
    nq = NA_G * GRID_W
    us = jnp.clip(NA_G * g - NA_KH // 2, 0, rows - NA_SLAB)
    start = pl.multiple_of(us * GRID_W, GRID_W)

    entry, ok_l, ok_r = {}, {}, {}
    for dr in range(NA_G):
        r = NA_G * g + dr
        rs = jnp.clip(r - NA_KH // 2, 0, rows - NA_KH)
        for p in range(NA_SLAB // 2):
            kr = us + 2 * p
            l_ok = jnp.logical_and(kr >= rs, kr < rs + NA_KH)
            r_ok = jnp.logical_and(kr + 1 >= rs, kr + 1 < rs + NA_KH)
            ro = kr - r + (NA_KH - 1)
            entry[dr, p] = jnp.where(jnp.logical_or(l_ok, r_ok), jnp.clip(ro + 1, 0, NA_NEG_ENTRY - 1),
                                     NA_NEG_ENTRY)
            ok_l[dr, p] = l_ok.astype(jnp.int32)
            ok_r[dr, p] = r_ok.astype(jnp.int32)
    lane = lax.broadcasted_iota(jnp.int32, (GRID_W, 128), 1)

    def bias_tile(hp, sub, dr, p):
        t = b_ref[hp, entry[dr, p], sub * GRID_W:(sub + 1) * GRID_W, :]
        if dr % 2 == 1 and p in ((dr - 1) // 2, (dr + NA_KH - 1) // 2):
            t = jnp.where(jnp.where(lane < GRID_W, ok_l[dr, p], ok_r[dr, p]) > 0, t, NEG)
        return t

    def scores(hp):
        cols = slice(hp * 128, (hp + 1) * 128)
        qm = _pair_rows(q_ref[qrows, cols] * NA_SCALE)
        k2 = k_ref[pl.ds(start, NA_SLAB * GRID_W), cols]
        bias = jnp.concatenate(
            [jnp.concatenate([bias_tile(hp, sub, dr, p) for p in range(NA_SLAB // 2)], axis=1)
             for sub in range(2) for dr in range(NA_G)], axis=0)
        s_nb = _dot_nt(qm, k2) + bias
        s_cx = _dot_nt(qm, kc_ref[:, cols])
        return s_nb, s_cx

    def finish(hp, s_nb, s_cx):
        cols = slice(hp * 128, (hp + 1) * 128)
        v2 = v_ref[pl.ds(start, NA_SLAB * GRID_W), cols]
        m = jnp.maximum(jnp.max(s_nb, axis=-1, keepdims=True), jnp.max(s_cx, axis=-1, keepdims=True))
        p_nb = jnp.exp(s_nb - m)
        p_cx = jnp.exp(s_cx - m)
        den = jnp.sum(p_nb, axis=-1, keepdims=True) + jnp.sum(p_cx, axis=-1, keepdims=True)
        o = (_dot(p_nb.astype(BF16), v2) + _dot(p_cx.astype(BF16), vc_ref[:, cols])) / den
        o_ref[qrows, cols] = _unpair_rows(o, nq).astype(o_ref.dtype)

    pending = None
    for hp in range(NA_HEADS // 2):
        cur = (hp,) + scores(hp)
        if pending is not None:
            finish(*pending)
        pending = cur
    finish(*pending)


def _na_lat(p, pc, cq, ck, cv, cck, ccv, bias, n_batch, n_seq, n_ctx):
    rows = n_seq // GRID_W
    assert rows % (NA_G * NA_GPS) == 0 and rows >= NA_SLAB
    groups = rows // (NA_G * NA_GPS)
    nq = NA_GPS * NA_G * GRID_W
    return pl.pallas_call(
        _na_lat_kernel,
        out_shape=jax.ShapeDtypeStruct((n_batch * n_seq, NA_W), BF16),
        grid=(n_batch, groups),
        in_specs=[pl.BlockSpec((nq, NA_W), lambda b, g: (b * groups + g, cq // NA_W)),
                  pl.BlockSpec((n_seq, NA_W), lambda b, g: (b, ck // NA_W)),
                  pl.BlockSpec((n_seq, NA_W), lambda b, g: (b, cv // NA_W)),
                  pl.BlockSpec((n_ctx, NA_W), lambda b, g: (b, cck // NA_W)),
                  pl.BlockSpec((n_ctx, NA_W), lambda b, g: (b, ccv // NA_W)),
                  pl.BlockSpec((NA_HEADS // 2, NA_NEG_ENTRY + 1, 2 * GRID_W, 2 * GRID_W),
                               lambda b, g: (0, 0, 0, 0), pipeline_mode=pl.Buffered(1))],
        out_specs=pl.BlockSpec((nq, NA_W), lambda b, g: (b * groups + g, 0)),
        compiler_params=_cparams("arbitrary", "arbitrary"),
        name="na_lat",
    )(p, p, p, pc, pc, bias)


def _na_ctx_kernel(q_ref, kc_ref, vc_ref, o_ref):
    n = q_ref.shape[0]
    for hp in range(NA_HEADS // 2):
        cols = slice(hp * 128, (hp + 1) * 128)
        qm = _pair_rows(q_ref[:, cols] * NA_SCALE)
        s = _dot_nt(qm, kc_ref[:, cols])
        p = jnp.exp(s - jnp.max(s, axis=-1, keepdims=True))
        o = _dot(p.astype(BF16), vc_ref[:, cols]) / jnp.sum(p, axis=-1, keepdims=True)
        o_ref[:, cols] = _unpair_rows(o, n).astype(o_ref.dtype)


def _na_ctx(pc, cq, ck, cv, n_batch, n_ctx):
    return pl.pallas_call(
        _na_ctx_kernel,
        out_shape=jax.ShapeDtypeStruct((n_batch * n_ctx, NA_W), BF16),
        grid=(n_batch,),
        in_specs=[pl.BlockSpec((n_ctx, NA_W), lambda b: (b, cq // NA_W)),
                  pl.BlockSpec((n_ctx, NA_W), lambda b: (b, ck // NA_W)),
                  pl.BlockSpec((n_ctx, NA_W), lambda b: (b, cv // NA_W))],
        out_specs=pl.BlockSpec((n_ctx, NA_W), lambda b: (b, 0)),
        compiler_params=_cparams("arbitrary"),
        name="na_ctx",
    )(pc, pc, pc)


def _na_bias_table(rpb):
    c = np.arange(GRID_W)[:, None]
    kc = np.arange(GRID_W)[None, :]
    cs = np.clip(c - NA_KW // 2, 0, GRID_W - NA_KW)
    valid = (kc >= cs) & (kc < cs + NA_KW)
    co = kc - c + (NA_KW - 1)
    onehot = (valid[None] & (co[None] == np.arange(2 * NA_KW - 1)[:, None, None])).astype(np.float32)
    t = jnp.einsum('hrj,jck->hrck', rpb.astype(F32), onehot, precision=lax.Precision.HIGHEST)
    t = t + np.where(valid, 0.0, NEG).astype(np.float32)
    t = t.reshape(NA_HEADS // 2, 2, 2 * NA_KH - 1, GRID_W, GRID_W).transpose(0, 2, 1, 3, 4)
    t = t.reshape(NA_HEADS // 2, 2 * NA_KH - 1, 2 * GRID_W, GRID_W)
    neg = jnp.full((NA_HEADS // 2, 1, 2 * GRID_W, GRID_W), NEG, F32)
    t = jnp.concatenate([neg, t, neg, neg], axis=1)
    return jnp.concatenate([t[:, :-1], t[:, 1:]], axis=-1)


def _fourier_kernel(x_ref, cg_ref, sg_ref, ch_ref, sh_ref, j_ref, o_ref, *, norm):
    x = x_ref[...]
    n = x.shape[0]
    h = n // 2
    rb = j_ref.shape[0]
    jm = j_ref[...]

    def rev(t):
        m = t.shape[0]
        return jnp.concatenate([_dot(jm, t[m - (i + 1) * rb:m - i * rb, :]) for i in range(m // rb)], axis=0)

    def mirror(t, first_row):
        row = lax.broadcasted_iota(jnp.int32, t.shape, 0)
        return jnp.where(row == 0, first_row, pltpu.roll(rev(t), 1, 0))

    a = _dot(x, cg_ref[...])
    b = _dot(x, sg_ref[...])
    ta = mirror(a.astype(BF16), 0.0)
    tb = mirror(b.astype(BF16), 0.0)
    ae = (a + ta)[:h].astype(BF16)
    bo = (b - tb)[:h].astype(BF16)
    rowh = lax.broadcasted_iota(jnp.int32, (h, x.shape[1]), 0)
    sign_h = jnp.where(rowh % 2 == 0, 1.0, -1.0)
    e = _dot(ch_ref[...], ae) + sign_h * a[h:h + 1, :]
    o = _dot(sh_ref[...], bo)
    o_ref[:h, :] = ((e - o) * norm).astype(o_ref.dtype)
    rown = lax.broadcasted_iota(jnp.int32, a.shape, 0)
    mid = jnp.sum(jnp.where(rown % 2 == 0, a, -a), axis=0, keepdims=True)
    g = ((e + o) * norm).astype(BF16)
    o_ref[h:, :] = mirror(g, mid * norm).astype(o_ref.dtype)


@functools.lru_cache(maxsize=None)
def _dft_tables(n):
    j = np.arange(n, dtype=np.int64)
    ang = ((j[:, None] * j[None, :]) % n).astype(np.float64) * (2.0 * np.pi / n)
    return np.cos(ang).astype(np.float32), np.sin(ang).astype(np.float32)


def _fourier(p, n_batch, n_seq):
    h = n_seq // 2
    rb = min(256, h)
    assert h % rb == 0
    cg, sg = (jnp.asarray(t).astype(BF16) for t in _dft_tables(FNET_GW))
    ch, sh = (jnp.asarray(np.ascontiguousarray(t[:h, :h])).astype(BF16) for t in _dft_tables(n_seq))
    flip = jnp.asarray(np.eye(rb, dtype=np.float32)[::-1].copy()).astype(BF16)
    const = lambda b, g: (0, 0)
    return pl.pallas_call(
        functools.partial(_fourier_kernel, norm=float((n_seq * FNET_GW) ** -0.5)),
        out_shape=jax.ShapeDtypeStruct((n_batch * n_seq, FNET_W), BF16),
        grid=(n_batch, FNET_GROUPS),
        in_specs=[pl.BlockSpec((n_seq, FNET_GW), lambda b, g: (b, C_FV // FNET_GW + g)),
                  pl.BlockSpec((FNET_GW, FNET_GW), const),
                  pl.BlockSpec((FNET_GW, FNET_GW), const),
                  pl.BlockSpec((h, h), const, pipeline_mode=pl.Buffered(1)),
                  pl.BlockSpec((h, h), const, pipeline_mode=pl.Buffered(1)),
                  pl.BlockSpec((rb, rb), const)],
        out_specs=pl.BlockSpec((n_seq, FNET_GW), lambda b, g: (b, g)),
        compiler_params=_cparams("arbitrary", "arbitrary"),
        name="fourier",
    )(p, cg, sg, ch, sh, flip)


HALO = 16


def _combine_kernel(cbx_ref, ccp_ref, cxp_ref, ccn_ref, cxn_ref, cw_ref, o1_ref, o2_ref, o3_ref, g_ref, mg_ref,
                    w_ref, m_ref, *, tiles_per_seq):
    tm = m_ref.shape[0]
    pos = pl.program_id(0) % tiles_per_seq
    z = cbx_ref[:, CONV_W:2 * CONV_W].astype(F32) * cbx_ref[:, 2 * CONV_W:].astype(F32)
    z_before = ccp_ref[HALO - 1:HALO, :].astype(F32) * cxp_ref[HALO - 1:HALO, :].astype(F32)
    z_after = ccn_ref[0:1, :].astype(F32) * cxn_ref[0:1, :].astype(F32)
    z_before = jnp.where(pos == 0, 0.0, z_before)
    z_after = jnp.where(pos == tiles_per_seq - 1, 0.0, z_after)
    row = lax.broadcasted_iota(jnp.int32, z.shape, 0)
    zp = jnp.where(row == 0, z_before, pltpu.roll(z, 1, 0))
    zn = jnp.where(row == tm - 1, z_after, pltpu.roll(z, tm - 1, 0))
    cw = cw_ref[...]
    conv_o = cbx_ref[:, :CONV_W].astype(F32) * (zp * cw[0:1] + z * cw[1:2] + zn * cw[2:3])

    def project(k):
        o = conv_o if k == 0 else (o1_ref, o2_ref, o3_ref)[k - 1][...].astype(F32)
        g = g_ref[:, k * 1024:(k + 1) * 1024].astype(F32)
        return _dot((o * _silu(g)).astype(BF16), w_ref[k])

    acc = None
    y = project(0)
    for k in range(N_BRANCH):
        y_next = project(k + 1) if k + 1 < N_BRANCH else None
        r = y * jax.nn.sigmoid(mg_ref[:, k * D_MODEL:(k + 1) * D_MODEL].astype(F32))
        acc = r if acc is None else acc + r
        y = y_next
    m_ref[...] = acc.astype(m_ref.dtype)


def _combine(p, outs, conv_w, w_p, n_seq, tm):
    t = p.shape[0]
    assert n_seq % tm == 0 and tm % HALO == 0
    hb = tm // HALO
    last = t // HALO - 1
    o_spec = pl.BlockSpec((tm, 1024), lambda i: (i, 0))

    def halo(col, nxt):
        if nxt:
            return pl.BlockSpec((HALO, CONV_W), lambda i: (jnp.minimum((i + 1) * hb, last), col // CONV_W))
        return pl.BlockSpec((HALO, CONV_W), lambda i: (jnp.maximum(i * hb - 1, 0), col // CONV_W))

    return pl.pallas_call(
        functools.partial(_combine_kernel, tiles_per_seq=n_seq // tm),
        out_shape=jax.ShapeDtypeStruct((t, D_MODEL), BF16),
        grid=(t // tm,),
        in_specs=[pl.BlockSpec((tm, 3 * CONV_W), lambda i: (i, C_CB // (3 * CONV_W))),
                  halo(C_CC, False), halo(C_CX, False), halo(C_CC, True), halo(C_CX, True),
                  pl.BlockSpec((CONV_K, CONV_W), lambda i: (0, 0)),
                  o_spec, o_spec, o_spec,
                  pl.BlockSpec((tm, N_BRANCH * 1024), lambda i: (i, C_G // (N_BRANCH * 1024))),
                  pl.BlockSpec((tm, N_BRANCH * D_MODEL), lambda i: (i, C_MG // (N_BRANCH * D_MODEL))),
                  pl.BlockSpec((N_BRANCH, 1024, D_MODEL), lambda i: (0, 0, 0), pipeline_mode=pl.Buffered(1))],
        out_specs=pl.BlockSpec((tm, D_MODEL), lambda i: (i, 0)),
        compiler_params=_cparams("arbitrary"),
        name="combine",
    )(p, p, p, p, p, conv_w, *outs, p, p, w_p)


OUT_ROWS = 256


def _outproj_kernel(m_ref, w_ref, x_ref, gt_ref, g_ref, *rest, with_next):
    if with_next:
        sh_ref, sc_ref, gn_ref, o_ref, u_ref = rest
    else:
        (o_ref,) = rest
    def epilogue(r0, y):
        rows = slice(r0, r0 + OUT_ROWS)
        x_new = x_ref[rows, :] + gt_ref[0] * _rms(y, g_ref[...])
        if with_next:
            u_ref[rows, :] = (_rms(x_new, gn_ref[...]) * (1.0 + sc_ref[0]) + sh_ref[0]).astype(BF16)
        o_ref[rows, :] = x_new

    pending = None
    for r0 in range(0, m_ref.shape[0], OUT_ROWS):
        y = _dot(m_ref[r0:r0 + OUT_ROWS, :], w_ref[...])
        if pending is not None:
            epilogue(*pending)
        pending = (r0, y)
    epilogue(*pending)


def _outproj(m, w_out, x2, mod3, row_of_tile, g_post, tm, nxt=None):
    t = m.shape[0]
    tile = pl.BlockSpec((tm, D_MODEL), lambda i: (i, 0))
    vec = pl.BlockSpec((1, D_MODEL), lambda i: (0, 0))

    def mod_spec(part):
        return pl.BlockSpec((1, 1, D_MODEL), lambda i: (row_of_tile(i), 0, part))

    in_specs = [tile, pl.BlockSpec((D_MODEL, D_MODEL), lambda i: (0, 0), pipeline_mode=pl.Buffered(1)),
                tile, mod_spec(2), vec]
    args = [m, w_out, x2, mod3, g_post]
    out_shape = jax.ShapeDtypeStruct((t, D_MODEL), F32)
    out_specs = tile
    if nxt is not None:
        in_specs += [mod_spec(0), mod_spec(1), vec]
        args += [nxt[0], nxt[0], nxt[1]]
        out_shape = (out_shape, jax.ShapeDtypeStruct((t, D_MODEL), BF16))
        out_specs = (tile, tile)
    return pl.pallas_call(
        functools.partial(_outproj_kernel, with_next=nxt is not None),
        out_shape=out_shape,
        grid=(t // tm,),
        in_specs=in_specs,
        out_specs=out_specs,
        compiler_params=_cparams("arbitrary"),
        name="outproj",
    )(*args)


def _rope_swap(w):
    return jnp.concatenate([-w[..., 16:32], w[..., 0:16], -w[..., 48:64], w[..., 32:48]], axis=-1)


def _prep_w_uq(w):
    w3 = w.reshape(Q_LORA, MLA_HEADS, MLA_NOPE + MLA_ROPE)
    nope = w3[:, :, :MLA_NOPE].reshape(Q_LORA, MLA_W)
    r = w3[:, :, MLA_NOPE:]
    rr = jnp.concatenate([r, _rope_swap(r)], axis=-1).reshape(Q_LORA, MLA_HEADS * 128)
    return jnp.concatenate([nope, rr], axis=1).astype(BF16)


def _prep_w_ukv(w):
    w3 = w.reshape(KV_LORA, MLA_HEADS, MLA_NOPE + MLA_V)
    return (w3[:, :, :MLA_NOPE].reshape(KV_LORA, MLA_W).astype(BF16),
            w3[:, :, MLA_NOPE:].reshape(KV_LORA, MLA_W).T.astype(BF16))


def _rope_table(n):
    nf = MLA_ROPE // 4
    t = jnp.arange(n, dtype=jnp.int32)
    pos = jnp.stack([t // GRID_W, t % GRID_W], axis=-1).astype(F32)
    inv = ROPE_THETA ** (-jnp.arange(nf, dtype=F32) / nf)
    ang = pos[:, :, None] * inv
    cos = jnp.repeat(jnp.cos(ang), 2, axis=1).reshape(n, MLA_ROPE)
    sin = jnp.repeat(jnp.sin(ang), 2, axis=1).reshape(n, MLA_ROPE)
    return jnp.concatenate([cos, sin], axis=-1).astype(F32)


def _identity_rope_table(n):
    return jnp.concatenate([jnp.ones((n, MLA_ROPE), F32), jnp.zeros((n, MLA_ROPE), F32)], axis=-1)


def kernel(x, c, ctx, c_ctx, g_pre, g_post, w_ada, b_ada, w_in, g_q, g_kv, w_uq, w_ukv, conv_w, na_rpb,
           w_p_conv, w_p_mla, w_p_na, w_p_fnet, w_out):
    n_batch, n_seq, _ = x.shape
    n_ctx = ctx.shape[1]
    assert n_batch + 1 <= 8 and n_seq % 1024 == 0 and n_ctx % 256 == 0
    xl = x.reshape(n_batch * n_seq, D_MODEL)
    xc = ctx.reshape(n_batch * n_ctx, D_MODEL)

    cc = jnp.concatenate([c, c_ctx[None], jnp.zeros((8 - n_batch - 1, D_MODEL), c.dtype)], axis=0)
    mod = _ada(cc, w_ada, b_ada.reshape(DEPTH, 1, 3 * D_MODEL))

    wt_in = jnp.swapaxes(w_in, 1, 2)
    rope_l = _rope_table(n_seq)
    rope_c = _identity_rope_table(n_ctx)
    tm_l = min(2048, n_batch * n_seq)
    tm_c = n_batch * n_ctx
    ctx_row = lambda i: n_batch

    def lat_row_t(tm):
        return lambda i: i * tm // n_seq

    for l in range(DEPTH):
        ctx_out = l < DEPTH - 1
        mod3 = mod[l].reshape(8, 1, 3 * D_MODEL)
        w_q = _prep_w_uq(w_uq[l])
        w_k, w_vt = _prep_w_ukv(w_ukv[l])
        gq = g_q[l].reshape(1, Q_LORA)
        gkv = g_kv[l].reshape(1, KV_LORA)
        gpre = g_pre[l].reshape(1, D_MODEL)
        gpost = g_post[l].reshape(1, D_MODEL)
        w_p = jnp.stack([w_p_conv[l], w_p_mla[l], w_p_na[l], w_p_fnet[l]]).astype(BF16)
        w_o = w_out[l].astype(BF16)
        bias = _na_bias_table(na_rpb[l])

        all_tiles = tuple(range(NP_FULL // TN))
        if l == 0:
            ul = _prenorm(xl, mod3, lat_row_t(512), gpre, 512)
            uc = _prenorm(xc, mod3, ctx_row, gpre, min(512, tm_c))
        nxt = None
        if l + 1 < DEPTH:
            nxt = (mod[l + 1].reshape(8, 1, 3 * D_MODEL), g_pre[l + 1].reshape(1, D_MODEL))
        pl_ = _inproj(ul, wt_in, l, all_tiles, tm_l)
        if ctx_out:
            pc = _inproj(uc, wt_in, l, all_tiles, tm_c)
            base = 0
        else:
            pc = _inproj(uc, wt_in, l, all_tiles[KV_BASE // TN:], tm_c)
            base = KV_BASE

        kc_m, vc_m = _kvup(pc, C_KV - base, gkv, w_k, w_vt, rope_c, n_ctx, n_ctx)
        kl_m, vl_m = _kvup(pl_, C_KV, gkv, w_k, w_vt, rope_l, n_seq, 1024)
        ql_m = _qup(pl_, C_QC, gq, w_q, rope_l, n_seq, 1024)
        mla_l = _mla_attn(ql_m, kc_m, vc_m, kl_m, vl_m, n_batch, n_seq, n_ctx, min(2048, n_seq))
        na_l = _na_lat(pl_, pc, C_NQ, C_NK, C_NV, C_NK - base, C_NV - base, bias, n_batch, n_seq, n_ctx)
        fn_l = _fourier(pl_, n_batch, n_seq)
        m_l = _combine(pl_, (mla_l, na_l, fn_l), conv_w[l], w_p, n_seq, 256)
        xl_new = _outproj(m_l, w_o, xl, mod3, lat_row_t(512), gpost, 512, nxt)
        if nxt is not None:
            xl_new, ul = xl_new

        if ctx_out:
            qc_m = _qup(pc, C_QC, gq, w_q, rope_c, n_ctx, n_ctx)
            mla_c = _mla_attn(qc_m, kc_m, vc_m, None, None, n_batch, n_ctx, n_ctx, n_ctx)
            na_c = _na_ctx(pc, C_NQ, C_NK, C_NV, n_batch, n_ctx)
            fn_c = _fourier(pc, n_batch, n_ctx)
            m_c = _combine(pc, (mla_c, na_c, fn_c), conv_w[l], w_p, n_ctx, 256)
            xc, uc = _outproj(m_c, w_o, xc, mod3, ctx_row, gpost, min(512, tm_c), nxt)
        xl = xl_new

    return xl.reshape(n_batch, n_seq, D_MODEL)
```

```python
import functools

import jax
import jax.numpy as jnp
import numpy as np
from jax import lax
from jax.experimental import pallas as pl
from jax.experimental.pallas import tpu as pltpu

F32 = jnp.float32
BF16 = jnp.bfloat16

D_MODEL = 2048
DEPTH = 2
GRID_W = 64
CONV_W = 1024
CONV_K = 3
MLA_HEADS = 8
MLA_NOPE = 128
MLA_ROPE = 64
MLA_V = 128
Q_LORA = 512
KV_LORA = 256
MLA_W = MLA_HEADS * MLA_V
MLA_SCALE = (MLA_NOPE + MLA_ROPE) ** -0.5
LOG2E = 1.4426950408889634
NA_HEADS = 16
NA_HD = 64
NA_W = NA_HEADS * NA_HD
NA_KH = 8
NA_KW = 16
NA_SCALE = NA_HD ** -0.5
FNET_GROUPS = 4
FNET_GW = 256
FNET_W = FNET_GROUPS * FNET_GW
N_BRANCH = 4
ROPE_THETA = 10000.0
EPS = 1e-6
NEG = -1e30

VMEM_LIMIT_BYTES = 56 * 1024 * 1024

MLA_QK = 256
KV_SEG = 512
C_MG, C_G, C_CB, C_CC, C_CX, C_FV, C_NQ, C_NK, C_NV, C_QC, C_KV = (
    0, 8192, 12288, 13312, 14336, 15360, 16384, 17408, 18432, 19456, 19968)
NP_FULL = 20480
KV_BASE = C_NK


def _cparams(*sem):
    return pltpu.CompilerParams(dimension_semantics=sem, vmem_limit_bytes=VMEM_LIMIT_BYTES)


def _rms(x, g):
    return x * lax.rsqrt(jnp.mean(x * x, axis=-1, keepdims=True) + EPS) * g


def _silu(x):
    return x * jax.nn.sigmoid(x)


def _dot(a, b):
    return jnp.dot(a, b, preferred_element_type=F32)


def _dot_nt(a, b):
    return lax.dot_general(a, b, (((1,), (1,)), ((), ())), preferred_element_type=F32)


def _ada_kernel(c_ref, w_ref, b_ref, o_ref):
    a = _silu(c_ref[...]).astype(BF16)
    o_ref[0] = _dot(a, w_ref[0].astype(BF16)) + b_ref[0]


def _ada(cc, w_ada, b_ada):
    tn = 1024
    return pl.pallas_call(
        _ada_kernel,
        out_shape=jax.ShapeDtypeStruct((DEPTH, 8, 3 * D_MODEL), F32),
        grid=(DEPTH, 3 * D_MODEL // tn),
        in_specs=[pl.BlockSpec((8, D_MODEL), lambda l, j: (0, 0)),
                  pl.BlockSpec((1, D_MODEL, tn), lambda l, j: (l, 0, j)),
                  pl.BlockSpec((1, 1, tn), lambda l, j: (l, 0, j))],
        out_specs=pl.BlockSpec((1, 8, tn), lambda l, j: (l, 0, j)),
        compiler_params=_cparams("arbitrary", "arbitrary"),
        name="ada",
    )(cc, w_ada, b_ada)


def _prenorm_kernel(x_ref, sh_ref, sc_ref, g_ref, u_ref):
    y = _rms(x_ref[...], g_ref[...])
    u_ref[...] = (y * (1.0 + sc_ref[0]) + sh_ref[0]).astype(BF16)


def _prenorm(x2, mod3, row_of_tile, g_pre, tm):
    t = x2.shape[0]
    return pl.pallas_call(
        _prenorm_kernel,
        out_shape=jax.ShapeDtypeStruct((t, D_MODEL), BF16),
        grid=(t // tm,),
        in_specs=[pl.BlockSpec((tm, D_MODEL), lambda i: (i, 0)),
                  pl.BlockSpec((1, 1, D_MODEL), lambda i: (row_of_tile(i), 0, 0)),
                  pl.BlockSpec((1, 1, D_MODEL), lambda i: (row_of_tile(i), 0, 1)),
                  pl.BlockSpec((1, D_MODEL), lambda i: (0, 0))],
        out_specs=pl.BlockSpec((tm, D_MODEL), lambda i: (i, 0)),
        compiler_params=_cparams("arbitrary"),
        name="prenorm",
    )(x2, mod3, mod3, g_pre)


TN = 1024
W_ROWS = 256
N_ORIG = 20288

_TILE_SRC = ([12096 + TN * t for t in range(8)] + [8000 + TN * t for t in range(4)]
             + [3904 + TN * t for t in range(3)] + [6976, 2880, 320, 1344, -1])
SRC_QC = 2368
KV_ROWS = KV_LORA + MLA_ROPE


def _inproj_kernel(src_ref, u_ref, wt_hbm, o_ref, stage, wbuf, sems, *, layer, n_tiles):
    j = pl.program_id(0)

    def main_copy(jj):
        start = pl.multiple_of(src_ref[jj], 8)
        return pltpu.make_async_copy(wt_hbm.at[layer, pl.ds(start, TN), :], stage, sems.at[0])

    def tail_copies():
        return (pltpu.make_async_copy(wt_hbm.at[layer, pl.ds(SRC_QC, Q_LORA), :],
                                      stage.at[pl.ds(0, Q_LORA), :], sems.at[0]),
                pltpu.make_async_copy(wt_hbm.at[layer, pl.ds(0, KV_ROWS), :],
                                      stage.at[pl.ds(Q_LORA, KV_ROWS), :], sems.at[1]))

    def start(jj):
        @pl.when(src_ref[jj] >= 0)
        def _():
            main_copy(jj).start()

        @pl.when(src_ref[jj] < 0)
        def _():
            for c in tail_copies():
                c.start()

    def convert(n_rows):
        def body(r, carry):
            r0 = pl.multiple_of(r * W_ROWS, W_ROWS)
            wbuf[pl.ds(r0, W_ROWS), :] = stage[pl.ds(r0, W_ROWS), :].astype(BF16)
            return carry

        lax.fori_loop(0, n_rows // W_ROWS, body, 0)

    @pl.when(pl.program_id(1) == 0)
    def _():
        @pl.when(j == 0)
        def _():
            start(0)

        @pl.when(src_ref[j] >= 0)
        def _():
            main_copy(j).wait()
            convert(TN)

        @pl.when(src_ref[j] < 0)
        def _():
            for c in tail_copies():
                c.wait()
            convert(768)
            kr0 = Q_LORA + KV_LORA
            wbuf[768:832, :] = stage[768:832, :].astype(BF16)
            wbuf[832:848, :] = (-stage[kr0 + 16:kr0 + 32, :]).astype(BF16)
            wbuf[848:864, :] = stage[kr0:kr0 + 16, :].astype(BF16)
            wbuf[864:880, :] = (-stage[kr0 + 48:kr0 + 64, :]).astype(BF16)
            wbuf[880:896, :] = stage[kr0 + 32:kr0 + 48, :].astype(BF16)
            wbuf[896:TN, :] = jnp.zeros((TN - 896, D_MODEL), BF16)

        @pl.when(j + 1 < n_tiles)
        def _():
            start(j + 1)

    o_ref[...] = _dot_nt(u_ref[...], wbuf[...]).astype(o_ref.dtype)


def _inproj(u, wt_in, layer, tiles, tm):
    t = u.shape[0]
    src = np.array([_TILE_SRC[j] for j in tiles], np.int32)
    assert all(s % 8 == 0 and s + TN <= N_ORIG for s in src[src >= 0])
    grid_spec = pltpu.PrefetchScalarGridSpec(
        num_scalar_prefetch=1,
        grid=(len(tiles), t // tm),
        in_specs=[pl.BlockSpec((tm, D_MODEL), lambda j, i, s: (i, 0)),
                  pl.BlockSpec(memory_space=pl.ANY)],
        out_specs=pl.BlockSpec((tm, TN), lambda j, i, s: (i, j)),
        scratch_shapes=[pltpu.VMEM((TN, D_MODEL), F32), pltpu.VMEM((TN, D_MODEL), BF16),
                        pltpu.SemaphoreType.DMA((2,))])
    return pl.pallas_call(
        functools.partial(_inproj_kernel, layer=layer, n_tiles=len(tiles)),
        out_shape=jax.ShapeDtypeStruct((t, len(tiles) * TN), BF16),
        grid_spec=grid_spec,
        compiler_params=_cparams("arbitrary", "arbitrary"),
        name="inproj",
    )(jnp.asarray(src), u, wt_in)


def _qup_kernel(h_ref, g_ref, w_ref, cs_ref, o_ref):
    hn = _rms(h_ref[...].astype(F32), g_ref[...]).astype(BF16)
    qq = _dot(hn, w_ref[...]) * (MLA_SCALE * LOG2E)
    cs = cs_ref[...]
    for h in range(MLA_HEADS):
        o_ref[:, h * MLA_QK:h * MLA_QK + MLA_NOPE] = qq[:, h * 128:(h + 1) * 128].astype(BF16)
        t = qq[:, MLA_W + h * 128:MLA_W + (h + 1) * 128] * cs
        o_ref[:, h * MLA_QK + MLA_NOPE:(h + 1) * MLA_QK] = (t + pltpu.roll(t, 64, 1)).astype(BF16)


def _qup(p, col, g_q, w_q, cs, n_seq, tm):
    t = p.shape[0]
    spt = n_seq // tm
    return pl.pallas_call(
        _qup_kernel,
        out_shape=jax.ShapeDtypeStruct((t, MLA_HEADS * MLA_QK), BF16),
        grid=(t // tm,),
        in_specs=[pl.BlockSpec((tm, Q_LORA), lambda i: (i, col // Q_LORA)),
                  pl.BlockSpec((1, Q_LORA), lambda i: (0, 0)),
                  pl.BlockSpec((Q_LORA, 2 * MLA_W), lambda i: (0, 0)),
                  pl.BlockSpec((tm, 128), lambda i: (i % spt, 0))],
        out_specs=pl.BlockSpec((tm, MLA_HEADS * MLA_QK), lambda i: (i, 0)),
        compiler_params=_cparams("arbitrary"),
        name="mla_qup",
    )(p, g_q, w_q, cs)


def _kvup_kernel(p_ref, g_ref, wk_ref, wvt_ref, cs_ref, k_ref, vt_ref):
    p = p_ref[...]
    cn = _rms(p[:, :KV_LORA].astype(F32), g_ref[...]).astype(BF16)
    kn = _dot(cn, wk_ref[...])
    t = p[:, KV_LORA:KV_LORA + 128].astype(F32) * cs_ref[...]
    lane = lax.broadcasted_iota(jnp.int32, t.shape, 1)
    kr = jnp.where(lane < MLA_ROPE, t + pltpu.roll(t, 64, 1), 0.0).astype(BF16)
    for h in range(MLA_HEADS):
        k_ref[:, h * MLA_QK:h * MLA_QK + MLA_NOPE] = kn[:, h * 128:(h + 1) * 128].astype(BF16)
        k_ref[:, h * MLA_QK + MLA_NOPE:(h + 1) * MLA_QK] = kr
    vt_ref[0] = _dot_nt(wvt_ref[...], cn).astype(BF16)


def _kvup(p, col, g_kv, w_k, w_vt, cs, n_seq, tm):
    t = p.shape[0]
    spt = n_seq // tm
    return pl.pallas_call(
        _kvup_kernel,
        out_shape=(jax.ShapeDtypeStruct((t, MLA_HEADS * MLA_QK), BF16),
                   jax.ShapeDtypeStruct((t // n_seq, MLA_W, n_seq), BF16)),
        grid=(t // tm,),
        in_specs=[pl.BlockSpec((tm, KV_SEG), lambda i: (i, col // KV_SEG)),
                  pl.BlockSpec((1, KV_LORA), lambda i: (0, 0)),
                  pl.BlockSpec((KV_LORA, MLA_W), lambda i: (0, 0)),
                  pl.BlockSpec((MLA_W, KV_LORA), lambda i: (0, 0)),
                  pl.BlockSpec((tm, 128), lambda i: (i % spt, 0))],
        out_specs=(pl.BlockSpec((tm, MLA_HEADS * MLA_QK), lambda i: (i, 0)),
                   pl.BlockSpec((1, MLA_W, tm), lambda i: (i // spt, 0, i % spt))),
        compiler_params=_cparams("arbitrary"),
        name="mla_kvup",
    )(p, g_kv, w_k, w_vt, cs)


MLA_ROWS = 512
MLA_KEYS = 256
MLA_HPS = 4
MLA_AHEAD = 1


def _mla_attn_kernel(q_ref, kc_ref, vtc_ref, *rest, has_lat):
    if has_lat:
        kl_ref, vtl_ref, o_ref = rest
    else:
        (o_ref,) = rest
    def scores(hd, r0):
        qk = slice(hd * MLA_QK, (hd + 1) * MLA_QK)
        q = q_ref[r0:r0 + MLA_ROWS, qk]
        st_c = _dot_nt(kc_ref[:, qk], q)
        m = jnp.max(st_c, axis=0, keepdims=True)
        st_l = []
        if has_lat:
            for k0 in range(0, kl_ref.shape[0], MLA_KEYS):
                st = _dot_nt(kl_ref[k0:k0 + MLA_KEYS, qk], q)
                m = jnp.maximum(m, jnp.max(st, axis=0, keepdims=True))
                st_l.append(st)
        return st_c, st_l, m

    def finish(hd, r0, st_c, st_l, m):
        vv = slice(hd * MLA_V, (hd + 1) * MLA_V)
        pt_c = jnp.exp2(st_c - m)
        den = jnp.sum(pt_c, axis=0, keepdims=True)
        ot = _dot(vtc_ref[0, vv, :], pt_c.astype(BF16))
        for i, st in enumerate(st_l):
            pt = jnp.exp2(st - m)
            den = den + jnp.sum(pt, axis=0, keepdims=True)
            ot = ot + _dot(vtl_ref[0, vv, i * MLA_KEYS:(i + 1) * MLA_KEYS], pt.astype(BF16))
        o_ref[r0:r0 + MLA_ROWS, vv] = (ot / den).T.astype(o_ref.dtype)

    chains = [(hd, r0) for hd in range(MLA_HPS) for r0 in range(0, q_ref.shape[0], MLA_ROWS)]
    pending = []
    for chain in chains:
        pending.append(chain + scores(*chain))
        if len(pending) > MLA_AHEAD:
            finish(*pending.pop(0))
    for p in pending:
        finish(*p)


def _mla_attn(q, kc, vtc, kl, vtl, n_batch, n_q, n_ctx, tq):
    has_lat = kl is not None
    qt = n_q // tq
    qkw, vw = MLA_HPS * MLA_QK, MLA_HPS * MLA_V
    in_specs = [pl.BlockSpec((tq, qkw), lambda b, h, i: (b * qt + i, h)),
                pl.BlockSpec((n_ctx, qkw), lambda b, h, i: (b, h)),
                pl.BlockSpec((1, vw, n_ctx), lambda b, h, i: (b, h, 0))]
    args = [q, kc, vtc]
    if has_lat:
        in_specs += [pl.BlockSpec((n_q, qkw), lambda b, h, i: (b, h)),
                     pl.BlockSpec((1, vw, n_q), lambda b, h, i: (b, h, 0))]
        args += [kl, vtl]
    return pl.pallas_call(
        functools.partial(_mla_attn_kernel, has_lat=has_lat),
        out_shape=jax.ShapeDtypeStruct((n_batch * n_q, MLA_W), BF16),
        grid=(n_batch, MLA_HEADS // MLA_HPS, qt),
        in_specs=in_specs,
        out_specs=pl.BlockSpec((tq, vw), lambda b, h, i: (b * qt + i, h)),
        compiler_params=_cparams("arbitrary", "arbitrary", "arbitrary"),
        name="mla_attn_lat" if has_lat else "mla_attn_ctx",
    )(*args)


def _pair_rows(q2):
    lane = lax.broadcasted_iota(jnp.int32, q2.shape, 1)
    zero = jnp.zeros_like(q2)
    return jnp.concatenate([jnp.where(lane < NA_HD, q2, zero), jnp.where(lane >= NA_HD, q2, zero)], axis=0)


def _unpair_rows(o, n):
    lane = lax.broadcasted_iota(jnp.int32, (n, 128), 1)
    return jnp.where(lane < NA_HD, o[:n], o[n:])


NA_G = 4
NA_SLAB = NA_G + NA_KH
NA_NEG_ENTRY = 2 * NA_KH
NA_GPS = 2


def _na_lat_kernel(q_ref, k_ref, v_ref, kc_ref, vc_ref, b_ref, o_ref):
    pending = None
    for gi in range(NA_GPS):
        qrows = slice(gi * NA_G * GRID_W, (gi + 1) * NA_G * GRID_W)
        scores, finish = _na_group(pl.program_id(1) * NA_GPS + gi, qrows, q_ref, k_ref, v_ref, kc_ref, vc_ref,
                                   b_ref, o_ref)
        for hp in range(NA_HEADS // 2):
            cur = (finish, (hp,) + scores(hp))
            if pending is not None:
                pending[0](*pending[1])
            pending = cur
    pending[0](*pending[1])


def _na_group(g, qrows, q_ref, k_ref, v_ref, kc_ref, vc_ref, b_ref, o_ref):
    rows = k_ref.shape[0] // GRID_W
    nq = NA_G * GRID_W
    us = jnp.clip(NA_G * g - NA_KH // 2, 0, rows - NA_SLAB)
    start = pl.multiple_of(us * GRID_W, GRID_W)

    entry, ok_l, ok_r = {}, {}, {}
    for dr in range(NA_G):
        r = NA_G * g + dr
        rs = jnp.clip(r - NA_KH // 2, 0, rows - NA_KH)
        for p in range(NA_SLAB // 2):
            kr = us + 2 * p
            l_ok = jnp.logical_and(kr >= rs, kr < rs + NA_KH)
            r_ok = jnp.logical_and(kr + 1 >= rs, kr + 1 < rs + NA_KH)
            ro = kr - r + (NA_KH - 1)
            entry[dr, p] = jnp.where(jnp.logical_or(l_ok, r_ok), jnp.clip(ro + 1, 0, NA_NEG_ENTRY - 1),
                                     NA_NEG_ENTRY)
            ok_l[dr, p] = l_ok.astype(jnp.int32)
            ok_r[dr, p] = r_ok.astype(jnp.int32)
    lane = lax.broadcasted_iota(jnp.int32, (GRID_W, 128), 1)

    def bias_tile(hp, sub, dr, p):
        t = b_ref[hp, entry[dr, p], sub * GRID_W:(sub + 1) * GRID_W, :]
        if dr % 2 == 1 and p in ((dr - 1) // 2, (dr + NA_KH - 1) // 2):
            t = jnp.where(jnp.where(lane < GRID_W, ok_l[dr, p], ok_r[dr, p]) > 0, t, NEG)
        return t

    def scores(hp):
        cols = slice(hp * 128, (hp + 1) * 128)
        qm = _pair_rows(q_ref[qrows, cols] * NA_SCALE)
        k2 = k_ref[pl.ds(start, NA_SLAB * GRID_W), cols]
        bias = jnp.concatenate(
            [jnp.concatenate([bias_tile(hp, sub, dr, p) for p in range(NA_SLAB // 2)], axis=1)
             for sub in range(2) for dr in range(NA_G)], axis=0)
        s_nb = _dot_nt(qm, k2) + bias
        s_cx = _dot_nt(qm, kc_ref[:, cols])
        return s_nb, s_cx

    def finish(hp, s_nb, s_cx):
        cols = slice(hp * 128, (hp + 1) * 128)
        v2 = v_ref[pl.ds(start, NA_SLAB * GRID_W), cols]
        m = jnp.maximum(jnp.max(s_nb, axis=-1, keepdims=True), jnp.max(s_cx, axis=-1, keepdims=True))
        p_nb = jnp.exp(s_nb - m)
        p_cx = jnp.exp(s_cx - m)
        den = jnp.sum(p_nb, axis=-1, keepdims=True) + jnp.sum(p_cx, axis=-1, keepdims=True)
        o = (_dot(p_nb.astype(BF16), v2) + _dot(p_cx.astype(BF16), vc_ref[:, cols])) / den
        o_ref[qrows, cols] = _unpair_rows(o, nq).astype(o_ref.dtype)

    return scores, finish


def _na_lat(p, pc, cq, ck, cv, cck, ccv, bias, n_batch, n_seq, n_ctx):
    rows = n_seq // GRID_W
    assert rows % (NA_G * NA_GPS) == 0 and rows >= NA_SLAB
    groups = rows // (NA_G * NA_GPS)
    nq = NA_GPS * NA_G * GRID_W
    return pl.pallas_call(
        _na_lat_kernel,
        out_shape=jax.ShapeDtypeStruct((n_batch * n_seq, NA_W), BF16),
        grid=(n_batch, groups),
        in_specs=[pl.BlockSpec((nq, NA_W), lambda b, g: (b * groups + g, cq // NA_W)),
                  pl.BlockSpec((n_seq, NA_W), lambda b, g: (b, ck // NA_W)),
                  pl.BlockSpec((n_seq, NA_W), lambda b, g: (b, cv // NA_W)),
                  pl.BlockSpec((n_ctx, NA_W), lambda b, g: (b, cck // NA_W)),
                  pl.BlockSpec((n_ctx, NA_W), lambda b, g: (b, ccv // NA_W)),
                  pl.BlockSpec((NA_HEADS // 2, NA_NEG_ENTRY + 1, 2 * GRID_W, 2 * GRID_W),
                               lambda b, g: (0, 0, 0, 0), pipeline_mode=pl.Buffered(1))],
        out_specs=pl.BlockSpec((nq, NA_W), lambda b, g: (b * groups + g, 0)),
        compiler_params=_cparams("arbitrary", "arbitrary"),
        name="na_lat",
    )(p, p, p, pc, pc, bias)


def _na_ctx_kernel(q_ref, kc_ref, vc_ref, o_ref):
    n = q_ref.shape[0]
    for hp in range(NA_HEADS // 2):
        cols = slice(hp * 128, (hp + 1) * 128)
        qm = _pair_rows(q_ref[:, cols] * NA_SCALE)
        s = _dot_nt(qm, kc_ref[:, cols])
        p = jnp.exp(s - jnp.max(s, axis=-1, keepdims=True))
        o = _dot(p.astype(BF16), vc_ref[:, cols]) / jnp.sum(p, axis=-1, keepdims=True)
        o_ref[:, cols] = _unpair_rows(o, n).astype(o_ref.dtype)


def _na_ctx(pc, cq, ck, cv, n_batch, n_ctx):
    return pl.pallas_call(
        _na_ctx_kernel,
        out_shape=jax.ShapeDtypeStruct((n_batch * n_ctx, NA_W), BF16),
        grid=(n_batch,),
        in_specs=[pl.BlockSpec((n_ctx, NA_W), lambda b: (b, cq // NA_W)),
                  pl.BlockSpec((n_ctx, NA_W), lambda b: (b, ck // NA_W)),
                  pl.BlockSpec((n_ctx, NA_W), lambda b: (b, cv // NA_W))],
        out_specs=pl.BlockSpec((n_ctx, NA_W), lambda b: (b, 0)),
        compiler_params=_cparams("arbitrary"),
        name="na_ctx",
    )(pc, pc, pc)


def _na_bias_table(rpb):
    c = np.arange(GRID_W)[:, None]
    kc = np.arange(GRID_W)[None, :]
    cs = np.clip(c - NA_KW // 2, 0, GRID_W - NA_KW)
    valid = (kc >= cs) & (kc < cs + NA_KW)
    co = kc - c + (NA_KW - 1)
    onehot = (valid[None] & (co[None] == np.arange(2 * NA_KW - 1)[:, None, None])).astype(np.float32)
    t = jnp.einsum('hrj,jck->hrck', rpb.astype(F32), onehot, precision=lax.Precision.HIGHEST)
    t = t + np.where(valid, 0.0, NEG).astype(np.float32)
    t = t.reshape(NA_HEADS // 2, 2, 2 * NA_KH - 1, GRID_W, GRID_W).transpose(0, 2, 1, 3, 4)
    t = t.reshape(NA_HEADS // 2, 2 * NA_KH - 1, 2 * GRID_W, GRID_W)
    neg = jnp.full((NA_HEADS // 2, 1, 2 * GRID_W, GRID_W), NEG, F32)
    t = jnp.concatenate([neg, t, neg, neg], axis=1)
    return jnp.concatenate([t[:, :-1], t[:, 1:]], axis=-1)


def _fourier_kernel(x_ref, cg_ref, sg_ref, ch_ref, sh_ref, j_ref, o_ref, *, norm):
    x = x_ref[...]
    n = x.shape[0]
    h = n // 2
    rb = j_ref.shape[0]
    jm = j_ref[...]

    def rev(t):
        m = t.shape[0]
        return jnp.concatenate([_dot(jm, t[m - (i + 1) * rb:m - i * rb, :]) for i in range(m // rb)], axis=0)

    def mirror(t, first_row):
        row = lax.broadcasted_iota(jnp.int32, t.shape, 0)
        return jnp.where(row == 0, first_row, pltpu.roll(rev(t), 1, 0))

    a = _dot(x, cg_ref[...])
    b = _dot(x, sg_ref[...])
    ta = mirror(a.astype(BF16), 0.0)
    tb = mirror(b.astype(BF16), 0.0)
    ae = (a + ta)[:h].astype(BF16)
    bo = (b - tb)[:h].astype(BF16)
    rowh = lax.broadcasted_iota(jnp.int32, (h, x.shape[1]), 0)
    sign_h = jnp.where(rowh % 2 == 0, 1.0, -1.0)
    e = _dot(ch_ref[...], ae) + sign_h * a[h:h + 1, :]
    o = _dot(sh_ref[...], bo)
    o_ref[:h, :] = ((e - o) * norm).astype(o_ref.dtype)
    rown = lax.broadcasted_iota(jnp.int32, a.shape, 0)
    mid = jnp.sum(jnp.where(rown % 2 == 0, a, -a), axis=0, keepdims=True)
    g = ((e + o) * norm).astype(BF16)
    o_ref[h:, :] = mirror(g, mid * norm).astype(o_ref.dtype)


@functools.lru_cache(maxsize=None)
def _dft_tables(n):
    j = np.arange(n, dtype=np.int64)
    ang = ((j[:, None] * j[None, :]) % n).astype(np.float64) * (2.0 * np.pi / n)
    return np.cos(ang).astype(np.float32), np.sin(ang).astype(np.float32)


def _fourier(p, n_batch, n_seq):
    h = n_seq // 2
    rb = min(256, h)
    assert h % rb == 0
    cg, sg = (jnp.asarray(t).astype(BF16) for t in _dft_tables(FNET_GW))
    ch, sh = (jnp.asarray(np.ascontiguousarray(t[:h, :h])).astype(BF16) for t in _dft_tables(n_seq))
    flip = jnp.asarray(np.eye(rb, dtype=np.float32)[::-1].copy()).astype(BF16)
    const = lambda b, g: (0, 0)
    return pl.pallas_call(
        functools.partial(_fourier_kernel, norm=float((n_seq * FNET_GW) ** -0.5)),
        out_shape=jax.ShapeDtypeStruct((n_batch * n_seq, FNET_W), BF16),
        grid=(n_batch, FNET_GROUPS),
        in_specs=[pl.BlockSpec((n_seq, FNET_GW), lambda b, g: (b, C_FV // FNET_GW + g)),
                  pl.BlockSpec((FNET_GW, FNET_GW), const),
                  pl.BlockSpec((FNET_GW, FNET_GW), const),
                  pl.BlockSpec((h, h), const, pipeline_mode=pl.Buffered(1)),
                  pl.BlockSpec((h, h), const, pipeline_mode=pl.Buffered(1)),
                  pl.BlockSpec((rb, rb), const)],
        out_specs=pl.BlockSpec((n_seq, FNET_GW), lambda b, g: (b, g)),
        compiler_params=_cparams("arbitrary", "arbitrary"),
        name="fourier",
    )(p, cg, sg, ch, sh, flip)


HALO = 16


def _combine_kernel(cbx_ref, ccp_ref, cxp_ref, ccn_ref, cxn_ref, cw_ref, o1_ref, o2_ref, o3_ref, g_ref, mg_ref,
                    w_ref, m_ref, *, tiles_per_seq):
    tm = m_ref.shape[0]
    pos = pl.program_id(0) % tiles_per_seq
    z = cbx_ref[:, CONV_W:2 * CONV_W].astype(F32) * cbx_ref[:, 2 * CONV_W:].astype(F32)
    z_before = ccp_ref[HALO - 1:HALO, :].astype(F32) * cxp_ref[HALO - 1:HALO, :].astype(F32)
    z_after = ccn_ref[0:1, :].astype(F32) * cxn_ref[0:1, :].astype(F32)
    z_before = jnp.where(pos == 0, 0.0, z_before)
    z_after = jnp.where(pos == tiles_per_seq - 1, 0.0, z_after)
    row = lax.broadcasted_iota(jnp.int32, z.shape, 0)
    zp = jnp.where(row == 0, z_before, pltpu.roll(z, 1, 0))
    zn = jnp.where(row == tm - 1, z_after, pltpu.roll(z, tm - 1, 0))
    cw = cw_ref[...]
    conv_o = cbx_ref[:, :CONV_W].astype(F32) * (zp * cw[0:1] + z * cw[1:2] + zn * cw[2:3])

    def project(k):
        o = conv_o if k == 0 else (o1_ref, o2_ref, o3_ref)[k - 1][...].astype(F32)
        g = g_ref[:, k * 1024:(k + 1) * 1024].astype(F32)
        return _dot((o * _silu(g)).astype(BF16), w_ref[k])

    acc = None
    y = project(0)
    for k in range(N_BRANCH):
        y_next = project(k + 1) if k + 1 < N_BRANCH else None
        r = y * jax.nn.sigmoid(mg_ref[:, k * D_MODEL:(k + 1) * D_MODEL].astype(F32))
        acc = r if acc is None else acc + r
        y = y_next
    m_ref[...] = acc.astype(m_ref.dtype)


def _combine(p, outs, conv_w, w_p, n_seq, tm):
    t = p.shape[0]
    assert n_seq % tm == 0 and tm % HALO == 0
    hb = tm // HALO
    last = t // HALO - 1
    o_spec = pl.BlockSpec((tm, 1024), lambda i: (i, 0))

    def halo(col, nxt):
        if nxt:
            return pl.BlockSpec((HALO, CONV_W), lambda i: (jnp.minimum((i + 1) * hb, last), col // CONV_W))
        return pl.BlockSpec((HALO, CONV_W), lambda i: (jnp.maximum(i * hb - 1, 0), col // CONV_W))

    return pl.pallas_call(
        functools.partial(_combine_kernel, tiles_per_seq=n_seq // tm),
        out_shape=jax.ShapeDtypeStruct((t, D_MODEL), BF16),
        grid=(t // tm,),
        in_specs=[pl.BlockSpec((tm, 3 * CONV_W), lambda i: (i, C_CB // (3 * CONV_W))),
                  halo(C_CC, False), halo(C_CX, False), halo(C_CC, True), halo(C_CX, True),
                  pl.BlockSpec((CONV_K, CONV_W), lambda i: (0, 0)),
                  o_spec, o_spec, o_spec,
                  pl.BlockSpec((tm, N_BRANCH * 1024), lambda i: (i, C_G // (N_BRANCH * 1024))),
                  pl.BlockSpec((tm, N_BRANCH * D_MODEL), lambda i: (i, C_MG // (N_BRANCH * D_MODEL))),
                  pl.BlockSpec((N_BRANCH, 1024, D_MODEL), lambda i: (0, 0, 0), pipeline_mode=pl.Buffered(1))],
        out_specs=pl.BlockSpec((tm, D_MODEL), lambda i: (i, 0)),
        compiler_params=_cparams("arbitrary"),
        name="combine",
    )(p, p, p, p, p, conv_w, *outs, p, p, w_p)


OUT_ROWS = 256


def _outproj_kernel(m_ref, w_ref, x_ref, gt_ref, g_ref, *rest, with_next):
    if with_next:
        sh_ref, sc_ref, gn_ref, o_ref, u_ref = rest
    else:
        (o_ref,) = rest
    def epilogue(r0, y):
        rows = slice(r0, r0 + OUT_ROWS)
        x_new = x_ref[rows, :] + gt_ref[0] * _rms(y, g_ref[...])
        if with_next:
            u_ref[rows, :] = (_rms(x_new, gn_ref[...]) * (1.0 + sc_ref[0]) + sh_ref[0]).astype(BF16)
        o_ref[rows, :] = x_new

    pending = None
    for r0 in range(0, m_ref.shape[0], OUT_ROWS):
        y = _dot(m_ref[r0:r0 + OUT_ROWS, :], w_ref[...])
        if pending is not None:
            epilogue(*pending)
        pending = (r0, y)
    epilogue(*pending)


def _outproj(m, w_out, x2, mod3, row_of_tile, g_post, tm, nxt=None):
    t = m.shape[0]
    tile = pl.BlockSpec((tm, D_MODEL), lambda i: (i, 0))
    vec = pl.BlockSpec((1, D_MODEL), lambda i: (0, 0))

    def mod_spec(part):
        return pl.BlockSpec((1, 1, D_MODEL), lambda i: (row_of_tile(i), 0, part))

    in_specs = [tile, pl.BlockSpec((D_MODEL, D_MODEL), lambda i: (0, 0), pipeline_mode=pl.Buffered(1)),
                tile, mod_spec(2), vec]
    args = [m, w_out, x2, mod3, g_post]
    out_shape = jax.ShapeDtypeStruct((t, D_MODEL), F32)
    out_specs = tile
    if nxt is not None:
        in_specs += [mod_spec(0), mod_spec(1), vec]
        args += [nxt[0], nxt[0], nxt[1]]
        out_shape = (out_shape, jax.ShapeDtypeStruct((t, D_MODEL), BF16))
        out_specs = (tile, tile)
    return pl.pallas_call(
        functools.partial(_outproj_kernel, with_next=nxt is not None),
        out_shape=out_shape,
        grid=(t // tm,),
        in_specs=in_specs,
        out_specs=out_specs,
        compiler_params=_cparams("arbitrary"),
        name="outproj",
    )(*args)


def _rope_swap(w):
    return jnp.concatenate([-w[..., 16:32], w[..., 0:16], -w[..., 48:64], w[..., 32:48]], axis=-1)


def _prep_w_uq(w):
    w3 = w.reshape(Q_LORA, MLA_HEADS, MLA_NOPE + MLA_ROPE)
    nope = w3[:, :, :MLA_NOPE].reshape(Q_LORA, MLA_W)
    r = w3[:, :, MLA_NOPE:]
    rr = jnp.concatenate([r, _rope_swap(r)], axis=-1).reshape(Q_LORA, MLA_HEADS * 128)
    return jnp.concatenate([nope, rr], axis=1).astype(BF16)


def _prep_w_ukv(w):
    w3 = w.reshape(KV_LORA, MLA_HEADS, MLA_NOPE + MLA_V)
    return (w3[:, :, :MLA_NOPE].reshape(KV_LORA, MLA_W).astype(BF16),
            w3[:, :, MLA_NOPE:].reshape(KV_LORA, MLA_W).T.astype(BF16))


def _rope_table(n):
    nf = MLA_ROPE // 4
    t = jnp.arange(n, dtype=jnp.int32)
    pos = jnp.stack([t // GRID_W, t % GRID_W], axis=-1).astype(F32)
    inv = ROPE_THETA ** (-jnp.arange(nf, dtype=F32) / nf)
    ang = pos[:, :, None] * inv
    cos = jnp.repeat(jnp.cos(ang), 2, axis=1).reshape(n, MLA_ROPE)
    sin = jnp.repeat(jnp.sin(ang), 2, axis=1).reshape(n, MLA_ROPE)
    return jnp.concatenate([cos, sin], axis=-1).astype(F32)


def _identity_rope_table(n):
    return jnp.concatenate([jnp.ones((n, MLA_ROPE), F32), jnp.zeros((n, MLA_ROPE), F32)], axis=-1)


def kernel(x, c, ctx, c_ctx, g_pre, g_post, w_ada, b_ada, w_in, g_q, g_kv, w_uq, w_ukv, conv_w, na_rpb,
           w_p_conv, w_p_mla, w_p_na, w_p_fnet, w_out):
    n_batch, n_seq, _ = x.shape
    n_ctx = ctx.shape[1]
    assert n_batch + 1 <= 8 and n_seq % 1024 == 0 and n_ctx % 256 == 0
    xl = x.reshape(n_batch * n_seq, D_MODEL)
    xc = ctx.reshape(n_batch * n_ctx, D_MODEL)

    cc = jnp.concatenate([c, c_ctx[None], jnp.zeros((8 - n_batch - 1, D_MODEL), c.dtype)], axis=0)
    mod = _ada(cc, w_ada, b_ada.reshape(DEPTH, 1, 3 * D_MODEL))

    wt_in = jnp.swapaxes(w_in, 1, 2)
    rope_l = _rope_table(n_seq)
    rope_c = _identity_rope_table(n_ctx)
    tm_l = min(2048, n_batch * n_seq)
    tm_c = n_batch * n_ctx
    ctx_row = lambda i: n_batch

    def lat_row_t(tm):
        return lambda i: i * tm // n_seq

    for l in range(DEPTH):
        ctx_out = l < DEPTH - 1
        mod3 = mod[l].reshape(8, 1, 3 * D_MODEL)
        w_q = _prep_w_uq(w_uq[l])
        w_k, w_vt = _prep_w_ukv(w_ukv[l])
        gq = g_q[l].reshape(1, Q_LORA)
        gkv = g_kv[l].reshape(1, KV_LORA)
        gpre = g_pre[l].reshape(1, D_MODEL)
        gpost = g_post[l].reshape(1, D_MODEL)
        w_p = jnp.stack([w_p_conv[l], w_p_mla[l], w_p_na[l], w_p_fnet[l]]).astype(BF16)
        w_o = w_out[l].astype(BF16)
        bias = _na_bias_table(na_rpb[l])

        all_tiles = tuple(range(NP_FULL // TN))
        if l == 0:
            ul = _prenorm(xl, mod3, lat_row_t(512), gpre, 512)
            uc = _prenorm(xc, mod3, ctx_row, gpre, min(512, tm_c))
        nxt = None
        if l + 1 < DEPTH:
            nxt = (mod[l + 1].reshape(8, 1, 3 * D_MODEL), g_pre[l + 1].reshape(1, D_MODEL))
        pl_ = _inproj(ul, wt_in, l, all_tiles, tm_l)
        if ctx_out:
            pc = _inproj(uc, wt_in, l, all_tiles, tm_c)
            base = 0
        else:
            pc = _inproj(uc, wt_in, l, all_tiles[KV_BASE // TN:], tm_c)
            base = KV_BASE

        kc_m, vc_m = _kvup(pc, C_KV - base, gkv, w_k, w_vt, rope_c, n_ctx, n_ctx)
        kl_m, vl_m = _kvup(pl_, C_KV, gkv, w_k, w_vt, rope_l, n_seq, 1024)
        ql_m = _qup(pl_, C_QC, gq, w_q, rope_l, n_seq, 1024)
        mla_l = _mla_attn(ql_m, kc_m, vc_m, kl_m, vl_m, n_batch, n_seq, n_ctx, min(2048, n_seq))
        na_l = _na_lat(pl_, pc, C_NQ, C_NK, C_NV, C_NK - base, C_NV - base, bias, n_batch, n_seq, n_ctx)
        fn_l = _fourier(pl_, n_batch, n_seq)
        m_l = _combine(pl_, (mla_l, na_l, fn_l), conv_w[l], w_p, n_seq, 256)
        xl_new = _outproj(m_l, w_o, xl, mod3, lat_row_t(512), gpost, 512, nxt)
        if nxt is not None:
            xl_new, ul = xl_new

        if ctx_out:
            qc_m = _qup(pc, C_QC, gq, w_q, rope_c, n_ctx, n_ctx)
            mla_c = _mla_attn(qc_m, kc_m, vc_m, None, None, n_batch, n_ctx, n_ctx, n_ctx)
            na_c = _na_ctx(pc, C_NQ, C_NK, C_NV, n_batch, n_ctx)
            fn_c = _fourier(pc, n_batch, n_ctx)
            m_c = _combine(pc, (mla_c, na_c, fn_c), conv_w[l], w_p, n_ctx, 256)
            xc, uc = _outproj(m_c, w_o, xc, mod3, ctx_row, gpost, min(512, tm_c), nxt)
        xl = xl_new

    return xl.reshape(n_batch, n_seq, D_MODEL)
```
